```python
import jax, jax.numpy as jnp
from jax import lax
import numpy as np

D_MODEL = 1024
BATCH = 16
SEQ = 2048
DEPTH = 2

CTX_LEN = 256
GRID_W = 64
HEAD_DIM = 64
N_Q_HEADS = 4
N_KV_HEADS = 2
Q_PER_KV = N_Q_HEADS // N_KV_HEADS
Q_W = N_Q_HEADS * HEAD_DIM
KV_W = N_KV_HEADS * HEAD_DIM
FOURIER_GROUPS = 4
FOURIER_GROUP_W = 64
F_W = FOURIER_GROUPS * FOURIER_GROUP_W
N_BRANCH = 4
BRANCH_W = 256
IN_W = 3 * (Q_W + 2 * KV_W) + F_W + N_BRANCH * D_MODEL
Q_BLOCK = 128
WINDOW = 128
NA_KH_MAX = 8
NA_KW = 16
ROPE_THETA = 10000.0
N_EXPERTS = 16
EXPERT_FF = 1024
EC_CAPACITY = 2
N_MOD = 6
DN_ALPHA = (2 * DEPTH) ** 0.25
DN_BETA = (8 * DEPTH) ** -0.25
LN_EPS = 1e-6
RMS_EPS = 1e-6
NEG_INF = -1e30

kernel_name = "hybrid_gated_mixers_ec_moe_dit"


def layer_norm(h, g, b):
    hf = h.astype(jnp.float32)
    mu = jnp.mean(hf, -1, keepdims=True)
    var = jnp.mean(jnp.square(hf - mu), -1, keepdims=True)
    return ((hf - mu) * lax.rsqrt(var + LN_EPS)).astype(h.dtype) * g + b


def rms_norm(h, g):
    hf = h.astype(jnp.float32)
    return (hf * lax.rsqrt(jnp.mean(hf * hf, -1, keepdims=True) + RMS_EPS)).astype(h.dtype) * g


def heads(t, n):
    return t.reshape(t.shape[:-1] + (n, HEAD_DIM))


def group_q(q):
    return q.reshape(q.shape[:-2] + (N_KV_HEADS, Q_PER_KV, HEAD_DIM))


def flat_heads(o):
    return o.reshape(o.shape[:2] + (Q_W,))


def split_columns(p):
    widths = (Q_W, KV_W, KV_W, F_W, Q_W, KV_W, KV_W, Q_W, KV_W, KV_W)
    offs, o = [], 0
    for w in widths:
        o += w
        offs.append(o)
    return jnp.split(p, offs, axis=-1)


def axial_rope_tables(n_tokens, dtype):
    t = jnp.arange(n_tokens)
    axis_dim = HEAD_DIM // 2
    inv = ROPE_THETA ** (-jnp.arange(0, axis_dim, 2, dtype=jnp.float32) / axis_dim)
    out = []
    for pos in (t // GRID_W, t % GRID_W):
        ang = pos.astype(jnp.float32)[:, None, None] * inv
        out.append(jnp.cos(ang).astype(dtype))
        out.append(jnp.sin(ang).astype(dtype))
    return tuple(out)


def rotate(t, cos, sin):
    h = t.shape[-1] // 2
    t1, t2 = t[..., :h], t[..., h:]
    return jnp.concatenate([t1 * cos - t2 * sin, t1 * sin + t2 * cos], -1)


def axial_rope(t, rope):
    cos_r, sin_r, cos_c, sin_c = rope
    half = HEAD_DIM // 2
    return jnp.concatenate([rotate(t[..., :half], cos_r, sin_r),
                            rotate(t[..., half:], cos_c, sin_c)], -1)


def neighbourhood_tables(n_tokens):
    rows = n_tokens // GRID_W
    kh = min(NA_KH_MAX, rows)
    t = jnp.arange(n_tokens)
    r, col = t // GRID_W, t % GRID_W
    r0 = jnp.clip(r - kh // 2, 0, rows - kh)
    c0 = jnp.clip(col - NA_KW // 2, 0, GRID_W - NA_KW)
    kr = r0[:, None, None] + jnp.arange(kh)[None, :, None]
    kc = c0[:, None, None] + jnp.arange(NA_KW)[None, None, :]
    shape = (n_tokens, kh, NA_KW)
    idx = jnp.broadcast_to(kr * GRID_W + kc, shape).reshape(n_tokens, -1)
    off_r = jnp.broadcast_to(kr - r[:, None, None] + NA_KH_MAX - 1, shape).reshape(n_tokens, -1)
    off_c = jnp.broadcast_to(kc - col[:, None, None] + NA_KW - 1, shape).reshape(n_tokens, -1)
    return idx, off_r, off_c


def gqa_softmax(q, k, v, bias=None, sink=None):
    s = jnp.einsum("bqhgd,bkhd->bhgqk", q, k).astype(jnp.float32) * HEAD_DIM ** -0.5
    if bias is not None:
        s = s + bias
    if sink is not None:
        sk = jnp.broadcast_to(sink.astype(jnp.float32)[None, :, :, None, None], s.shape[:-1] + (1,))
        p = jax.nn.softmax(jnp.concatenate([s, sk], -1), axis=-1)[..., :-1]
    else:
        p = jax.nn.softmax(s, axis=-1)
    return jnp.einsum("bhgqk,bkhd->bqhgd", p.astype(v.dtype), v)


def to_blocks(q):
    B, S = q.shape[:2]
    return jnp.swapaxes(q.reshape((B, S // Q_BLOCK, Q_BLOCK) + q.shape[2:]), 0, 1)


def from_blocks(ob, like):
    return flat_heads(jnp.swapaxes(ob, 0, 1).reshape(like.shape))


def global_attention(q, k_all, v_all):
    ob = lax.map(lambda qq: gqa_softmax(qq, k_all, v_all), to_blocks(q))
    return from_blocks(ob, q)


def window_attention(q, k_lat, v_lat, k_ctx, v_ctx, sink):
    S = q.shape[1]
    nblk = S // Q_BLOCK
    span = Q_BLOCK + 2 * WINDOW
    pad = ((0, 0), (WINDOW, WINDOW), (0, 0), (0, 0))
    k_pad, v_pad = jnp.pad(k_lat, pad), jnp.pad(v_lat, pad)
    ctx_bias = jnp.zeros((Q_BLOCK, k_ctx.shape[1]), jnp.float32)

    def one_block(args):
        qq, blk = args
        start = blk * Q_BLOCK
        kk = lax.dynamic_slice_in_dim(k_pad, start, span, axis=1)
        vv = lax.dynamic_slice_in_dim(v_pad, start, span, axis=1)
        q_pos = start + jnp.arange(Q_BLOCK)
        k_pos = start - WINDOW + jnp.arange(span)
        ok = (jnp.abs(q_pos[:, None] - k_pos[None, :]) <= WINDOW) & (k_pos >= 0) & (k_pos < S)
        bias = jnp.concatenate([ctx_bias, jnp.where(ok, 0.0, NEG_INF)], -1)
        return gqa_softmax(qq, jnp.concatenate([k_ctx, kk], 1), jnp.concatenate([v_ctx, vv], 1), bias, sink)

    ob = lax.map(one_block, (to_blocks(q), jnp.arange(nblk)))
    return from_blocks(ob, q)


def neighbourhood_attention(q, k_lat, v_lat, k_ctx, v_ctx, rpb, na_tab):
    idx, off_r, off_c = na_tab
    S = q.shape[1]
    nblk = S // Q_BLOCK
    n_nb = idx.shape[1]
    n_ctx = k_ctx.shape[1]
    scale = HEAD_DIM ** -0.5
    rpb_g = rpb.reshape((N_KV_HEADS, Q_PER_KV) + rpb.shape[1:]).astype(jnp.float32)
    bias = rpb_g[:, :, off_r, off_c]
    bias_b = jnp.moveaxis(bias.reshape(N_KV_HEADS, Q_PER_KV, nblk, Q_BLOCK, n_nb), 2, 0)
    idx_b = idx.reshape(nblk, Q_BLOCK, n_nb)

    def one_block(args):
        qq, ib, bb = args
        kk, vv = k_lat[:, ib], v_lat[:, ib]
        s_ctx = jnp.einsum("bqhgd,bchd->bhgqc", qq, k_ctx).astype(jnp.float32) * scale
        s_nb = jnp.einsum("bqhgd,bqkhd->bhgqk", qq, kk).astype(jnp.float32) * scale + bb
        p = jax.nn.softmax(jnp.concatenate([s_ctx, s_nb], -1), axis=-1).astype(v_lat.dtype)
        return (jnp.einsum("bhgqc,bchd->bqhgd", p[..., :n_ctx], v_ctx)
                + jnp.einsum("bhgqk,bqkhd->bqhgd", p[..., n_ctx:], vv))

    ob = lax.map(one_block, (to_blocks(q), idx_b, bias_b))
    return from_blocks(ob, q)


def fourier_mix(u):
    g = u.reshape(u.shape[:-1] + (FOURIER_GROUPS, FOURIER_GROUP_W)).astype(jnp.float32)
    f = jnp.fft.fft2(g, axes=(-3, -1), norm="ortho").real
    return f.reshape(u.shape).astype(u.dtype)


def merge_branches(branches, gate_logits, w_branch, w_out):
    gates = jax.nn.sigmoid(gate_logits)
    merged = gates[..., :D_MODEL] * (branches[0] @ w_branch[0])
    for i in range(1, N_BRANCH):
        merged = merged + gates[..., i * D_MODEL:(i + 1) * D_MODEL] * (branches[i] @ w_branch[i])
    return merged @ w_out


def token_mixers(u_lat, u_ctx, w_in, qk_gain, sink_logit, na_rpb, w_branch, w_out, rope, na_tab, with_ctx):
    pl = split_columns(u_lat @ w_in)
    pc = split_columns(u_ctx @ w_in)
    sink = sink_logit.reshape(N_KV_HEADS, Q_PER_KV)
    qa = group_q(axial_rope(rms_norm(heads(pl[0], N_Q_HEADS), qk_gain[0]), rope))
    ka = axial_rope(rms_norm(heads(pl[1], N_KV_HEADS), qk_gain[1]), rope)
    va = heads(pl[2], N_KV_HEADS)
    ka_c = rms_norm(heads(pc[1], N_KV_HEADS), qk_gain[1])
    va_c = heads(pc[2], N_KV_HEADS)
    br_a = global_attention(qa, jnp.concatenate([ka_c, ka], 1), jnp.concatenate([va_c, va], 1))
    br_b = fourier_mix(pl[3])
    qc = group_q(axial_rope(heads(pl[4], N_Q_HEADS), rope))
    kc = axial_rope(heads(pl[5], N_KV_HEADS), rope)
    vc = heads(pl[6], N_KV_HEADS)
    kc_c, vc_c = heads(pc[5], N_KV_HEADS), heads(pc[6], N_KV_HEADS)
    br_c = window_attention(qc, kc, vc, kc_c, vc_c, sink)
    qd = group_q(heads(pl[7], N_Q_HEADS))
    kd, vd = heads(pl[8], N_KV_HEADS), heads(pl[9], N_KV_HEADS)
    kd_c, vd_c = heads(pc[8], N_KV_HEADS), heads(pc[9], N_KV_HEADS)
    br_d = neighbourhood_attention(qd, kd, vd, kd_c, vd_c, na_rpb, na_tab)
    y_lat = merge_branches([br_a, br_b, br_c, br_d], pl[10], w_branch, w_out)
    if not with_ctx:
        return y_lat, None
    qa_c = group_q(rms_norm(heads(pc[0], N_Q_HEADS), qk_gain[0]))
    qc_c = group_q(heads(pc[4], N_Q_HEADS))
    qd_c = group_q(heads(pc[7], N_Q_HEADS))
    br_ctx = [flat_heads(gqa_softmax(qa_c, ka_c, va_c)),
              fourier_mix(pc[3]),
              flat_heads(gqa_softmax(qc_c, kc_c, vc_c, sink=sink)),
              flat_heads(gqa_softmax(qd_c, kd_c, vd_c))]
    y_ctx = merge_branches(br_ctx, pc[10], w_branch, w_out)
    return y_lat, y_ctx


def expert_choice_ffn(h, w_router, w_gate, w_up, w_down):
    B, n, D = h.shape
    cap = max(1, EC_CAPACITY * n // N_EXPERTS)
    aff = jax.nn.softmax(jnp.einsum("bnd,de->bne", h, w_router).astype(jnp.float32), axis=-1)
    sel_w, sel_idx = lax.top_k(jnp.swapaxes(aff, 1, 2), cap)
    xs = jax.vmap(lambda hb, ib: hb[ib])(h, sel_idx)
    a = jnp.einsum("becd,edf->becf", xs, w_gate)
    u = jnp.einsum("becd,edf->becf", xs, w_up)
    y = jnp.einsum("becf,efd->becd", jax.nn.silu(a) * u, w_down) * sel_w[..., None].astype(h.dtype)
    scatter = lambda yb, ib: jnp.zeros((n, D), h.dtype).at[ib.reshape(-1)].add(yb.reshape(-1, D))
    return jax.vmap(scatter)(y, sel_idx)


def trunk_layer(h_lat, h_ctx, c, c_ctx, w_mod, b_mod, w_in, qk_gain, sink_logit, na_rpb, w_branch,
                w_out, ln1_g, ln1_b, w_router, w_gate, w_up, w_down, ln2_g, ln2_b, rope, na_tab, with_ctx):
    m_lat = (jax.nn.silu(c) @ w_mod + b_mod)[:, None, :]
    m_ctx = jax.nn.silu(c_ctx) @ w_mod + b_mod
    sh1, sc1, g1, sh2, sc2, g2 = jnp.split(m_lat, N_MOD, -1)
    csh1, csc1, cg1, csh2, csc2, cg2 = jnp.split(m_ctx, N_MOD, -1)
    u_lat = h_lat * (1.0 + sc1) + sh1
    u_ctx = h_ctx * (1.0 + csc1) + csh1
    y_lat, y_ctx = token_mixers(u_lat, u_ctx, w_in, qk_gain, sink_logit, na_rpb, w_branch, w_out,
                                rope, na_tab, with_ctx)
    h_lat = layer_norm(DN_ALPHA * h_lat + g1 * y_lat, ln1_g, ln1_b)
    u_lat = h_lat * (1.0 + sc2) + sh2
    h_lat = layer_norm(DN_ALPHA * h_lat + g2 * expert_choice_ffn(u_lat, w_router, w_gate, w_up, w_down),
                       ln2_g, ln2_b)
    if with_ctx:
        h_ctx = layer_norm(DN_ALPHA * h_ctx + cg1 * y_ctx, ln1_g, ln1_b)
        u_ctx = h_ctx * (1.0 + csc2) + csh2
        h_ctx = layer_norm(DN_ALPHA * h_ctx + cg2 * expert_choice_ffn(u_ctx, w_router, w_gate, w_up, w_down),
                           ln2_g, ln2_b)
    return h_lat, h_ctx


def setup_inputs(seed: int = 0) -> dict:
    key = jax.random.key(seed)
    ks = jax.random.split(key, 22)
    L, D, E, F = DEPTH, D_MODEL, N_EXPERTS, EXPERT_FF

    def nrm(k, shape, s):
        return jax.random.normal(k, shape, jnp.float32) * s

    return {
        "x": nrm(ks[0], (BATCH, SEQ, D), 1.0),
        "c": nrm(ks[1], (BATCH, D), 1.0),
        "ctx": nrm(ks[2], (BATCH, CTX_LEN, D), 1.0),
        "c_ctx": nrm(ks[3], (D,), 1.0),
        "w_mod": nrm(ks[4], (L, D, N_MOD * D), 0.5 * D ** -0.5),
        "b_mod": nrm(ks[5], (L, N_MOD * D), 0.02),
        "w_in": nrm(ks[6], (L, D, IN_W), D ** -0.5),
        "qk_gain": 1.0 + nrm(ks[7], (L, 2, HEAD_DIM), 0.1),
        "sink_logit": nrm(ks[8], (L, N_Q_HEADS), 0.5),
        "na_rpb": nrm(ks[9], (L, N_Q_HEADS, 2 * NA_KH_MAX - 1, 2 * NA_KW - 1), 0.5),
        "w_branch": nrm(ks[10], (L, N_BRANCH, BRANCH_W, D), DN_BETA * BRANCH_W ** -0.5),
        "w_out": nrm(ks[11], (L, D, D), DN_BETA * D ** -0.5),
        "ln1_g": 1.0 + nrm(ks[12], (L, D), 0.1),
        "ln1_b": nrm(ks[13], (L, D), 0.02),
        "w_router": nrm(ks[14], (L, D, E), D ** -0.5),
        "w_gate": nrm(ks[15], (L, E, D, F), D ** -0.5),
        "w_up": nrm(ks[16], (L, E, D, F), D ** -0.5),
        "w_down": nrm(ks[17], (L, E, F, D), DN_BETA * F ** -0.5),
        "ln2_g": 1.0 + nrm(ks[18], (L, D), 0.1),
        "ln2_b": nrm(ks[19], (L, D), 0.02),
    }


def reference(x, c, ctx, c_ctx, w_mod, b_mod, w_in, qk_gain, sink_logit, na_rpb, w_branch, w_out,
              ln1_g, ln1_b, w_router, w_gate, w_up, w_down, ln2_g, ln2_b):
    n_tokens = x.shape[1]
    rope = axial_rope_tables(n_tokens, x.dtype)
    na_tab = neighbourhood_tables(n_tokens)
    h_lat, h_ctx = x, ctx
    for l in range(DEPTH):
        h_lat, h_ctx = trunk_layer(
            h_lat, h_ctx, c, c_ctx, w_mod[l], b_mod[l], w_in[l], qk_gain[l], sink_logit[l], na_rpb[l],
            w_branch[l], w_out[l], ln1_g[l], ln1_b[l], w_router[l], w_gate[l], w_up[l], w_down[l],
            ln2_g[l], ln2_b[l], rope, na_tab, l < DEPTH - 1)
    return h_lat
```

```python
import functools

import numpy as np
import jax
import jax.numpy as jnp
from jax import lax
from jax.experimental import pallas as pl
from jax.experimental.pallas import tpu as pltpu

D_MODEL = 1024
SEQ = 2048
DEPTH = 2
CTX_LEN = 256
GRID_W = 64
HEAD_DIM = 64
N_Q_HEADS = 4
N_KV_HEADS = 2
Q_W = 256
KV_W = 128
F_W = 256
FOURIER_GROUP_W = 64
N_BRANCH = 4
MIX_W = 3 * (Q_W + 2 * KV_W) + F_W
GATE_W = N_BRANCH * D_MODEL
WINDOW = 128
NA_KH = 8
NA_KW = 16
ROPE_THETA = 10000.0
N_EXPERTS = 16
EC_CAPACITY = 2
N_MOD = 6
DN_ALPHA = (2 * DEPTH) ** 0.25
LN_EPS = 1e-6
RMS_EPS = 1e-6
NEG_INF = -1e30

N_TOK = SEQ + CTX_LEN
TM = 256
QB = 128
N_LAT_T = SEQ // TM
N_LAT_QB = SEQ // QB
WIN_SPAN = QB + 2 * WINDOW
NA_SPAN = 640
NA_TYPES = 5
CAP_LAT = EC_CAPACITY * SEQ // N_EXPERTS
CAP_CTX = EC_CAPACITY * CTX_LEN // N_EXPERTS
R_LANES = 128

BF = jnp.bfloat16
F32 = jnp.float32
VMEM_LIMIT = 56 * 1024 * 1024


def _cparams(n_axes):
    return pltpu.CompilerParams(dimension_semantics=("arbitrary",) * n_axes,
                                vmem_limit_bytes=VMEM_LIMIT)


def _dot(a, b):
    return jnp.dot(a, b, preferred_element_type=F32)


def _dot_nt(a, b):
    return lax.dot_general(a, b, (((1,), (1,)), ((), ())), preferred_element_type=F32)


def _layer_norm(x, g, b):
    mu = jnp.mean(x, axis=-1, keepdims=True)
    xc = x - mu
    var = jnp.mean(xc * xc, axis=-1, keepdims=True)
    return xc * lax.rsqrt(var + LN_EPS) * g + b


def _mod_kernel(c_ref, w_ref, b_ref, o_ref):
    c = c_ref[...]
    a = c * jax.nn.sigmoid(c)
    o_ref[...] = jnp.dot(a, w_ref[...], precision=lax.Precision.HIGHEST,
                         preferred_element_type=F32) + b_ref[...]


def _modulation(c_all, w_mod, b_mod):
    L = w_mod.shape[0]
    R = c_all.shape[0]
    ncol = N_MOD * D_MODEL
    tn = D_MODEL
    return pl.pallas_call(
        _mod_kernel,
        grid=(L, ncol // tn),
        in_specs=[pl.BlockSpec((R, D_MODEL), lambda l, j: (0, 0)),
                  pl.BlockSpec((None, D_MODEL, tn), lambda l, j: (l, 0, j)),
                  pl.BlockSpec((None, 1, tn), lambda l, j: (l, 0, j))],
        out_specs=pl.BlockSpec((None, R, tn), lambda l, j: (l, 0, j)),
        out_shape=jax.ShapeDtypeStruct((L, R, ncol), F32),
        compiler_params=_cparams(2),
        name="modulation",
    )(c_all, w_mod, b_mod.reshape(L, 1, ncol))


def _rope128(x, cos, sin):
    lane = lax.broadcasted_iota(jnp.int32, x.shape, 1)
    first = (lane % 32) < 16
    up = pltpu.roll(x, 128 - 16, axis=1)
    dn = pltpu.roll(x, 16, axis=1)
    return x * cos + jnp.where(first, up, dn) * sin


def _rms128(x, gain, bd):
    x2 = x * x
    hi = x2.astype(BF)
    lo = (x2 - hi.astype(F32)).astype(BF)
    ms = _dot(hi, bd) + _dot(lo, bd)
    return x * lax.rsqrt(ms + RMS_EPS) * gain


def _q_slabs(x0, x1):
    lane = lax.broadcasted_iota(jnp.int32, x0.shape, 1)
    lo = lane < HEAD_DIM
    x0r = pltpu.roll(x0, HEAD_DIM, axis=1)
    x1r = pltpu.roll(x1, HEAD_DIM, axis=1)
    scale = HEAD_DIM ** -0.5
    slabs = [jnp.where(lo, x0, 0.0), jnp.where(lo, x0r, 0.0),
             jnp.where(lo, 0.0, x1r), jnp.where(lo, 0.0, x1)]
    return jnp.concatenate([(s * scale).astype(BF) for s in slabs], axis=1)


def _proj_kernel(hl_ref, hc_ref, mod_ref, w_ref, gq_ref, gk_ref, cos_ref, sin_ref, bd_ref,
                 qa_ref, ka_ref, va_ref, fb_ref, qc_ref, kc_ref, vc_ref, qd_ref, kd_ref, vd_ref):
    m = mod_ref[...]
    h = jnp.where(pl.program_id(1) == N_LAT_T, hc_ref[...], hl_ref[...])
    u = h * (1.0 + m[1:2]) + m[0:1]
    p = _dot(u.astype(BF), w_ref[...])
    cos = cos_ref[...]
    sin = sin_ref[...]
    bd = bd_ref[...]
    gq = gq_ref[...]
    gk = gk_ref[...]
    c0, s0 = cos[:, :128], sin[:, :128]
    c1, s1 = cos[:, 128:], sin[:, 128:]

    qa0 = _rope128(_rms128(p[:, 0:128], gq, bd), c0, s0)
    qa1 = _rope128(_rms128(p[:, 128:256], gq, bd), c1, s1)
    qa_ref[...] = _q_slabs(qa0, qa1)
    ka_ref[...] = _rope128(_rms128(p[:, 256:384], gk, bd), c0, s0).astype(BF)
    va_ref[...] = p[:, 384:512].astype(BF)
    fb_ref[...] = p[:, 512:768].astype(BF)
    qc_ref[...] = _q_slabs(_rope128(p[:, 768:896], c0, s0), _rope128(p[:, 896:1024], c1, s1))
    kc_ref[...] = _rope128(p[:, 1024:1152], c0, s0).astype(BF)
    vc_ref[...] = p[:, 1152:1280].astype(BF)
    qd_ref[...] = _q_slabs(p[:, 1280:1408], p[:, 1408:1536])
    kd_ref[...] = p[:, 1536:1664].astype(BF)
    vd_ref[...] = p[:, 1664:1792].astype(BF)


def _lat_ctx_specs(h_lat, h_ctx):
    lat = pl.BlockSpec((None, TM, D_MODEL), lambda b, t: (b, jnp.minimum(t, N_LAT_T - 1), 0))
    if h_ctx is None:
        return [lat, pl.BlockSpec((None, TM, D_MODEL), lambda b, t: (b, N_LAT_T, 0))], [h_lat, h_lat]
    return [lat, pl.BlockSpec((None, TM, D_MODEL), lambda b, t: (b, 0, 0))], [h_lat, h_ctx]


def _project(h_lat, h_ctx, mod, l, w_mix, gq, gk, cos_t, sin_t, bd):
    B = h_lat.shape[0]
    n_t = N_TOK // TM
    ctx_row = B
    tok = lambda w: pl.BlockSpec((None, TM, w), lambda b, t: (b, t, 0))
    const = lambda shape: pl.BlockSpec(shape, lambda b, t: (0,) * len(shape))
    widths = (2 * Q_W, KV_W, KV_W, F_W, 2 * Q_W, KV_W, KV_W, 2 * Q_W, KV_W, KV_W)
    h_specs, h_args = _lat_ctx_specs(h_lat, h_ctx)
    return pl.pallas_call(
        _proj_kernel,
        grid=(B, n_t),
        in_specs=h_specs + [
            pl.BlockSpec((None, None, N_MOD, D_MODEL),
                         lambda b, t: (l, jnp.where(t == N_LAT_T, ctx_row, b), 0, 0)),
            pl.BlockSpec((None, D_MODEL, MIX_W), lambda b, t: (l, 0, 0)),
            const((1, 128)), const((1, 128)),
            pl.BlockSpec((TM, Q_W), lambda b, t: (t, 0)),
            pl.BlockSpec((TM, Q_W), lambda b, t: (t, 0)),
            const((128, 128))],
        out_specs=[tok(w) for w in widths],
        out_shape=[jax.ShapeDtypeStruct((B, N_TOK, w), BF) for w in widths],
        compiler_params=_cparams(2),
        name="mixer_projection",
    )(*h_args, mod, w_mix, gq, gk, cos_t, sin_t, bd)


def _fourier_kernel(x_ref, cs_ref, cc_ref, sc_ref, o_ref, z_ref, *, row0, n):
    @pl.when(pl.program_id(1) == 0)
    def _():
        x = x_ref[row0:row0 + n, :]
        z_ref[0:n, :] = _dot(x, cc_ref[...]).astype(BF)
        z_ref[n:2 * n, :] = (-_dot(x, sc_ref[...])).astype(BF)

    scale = (n * FOURIER_GROUP_W) ** -0.5
    o_ref[...] = (_dot(cs_ref[...], z_ref[...]) * scale).astype(BF)


def _fourier(fb_all, cs, cc, sc, *, row0, n, tq):
    B = fb_all.shape[0]
    return pl.pallas_call(
        functools.partial(_fourier_kernel, row0=row0, n=n),
        grid=(B, n // tq),
        in_specs=[pl.BlockSpec((None, N_TOK, F_W), lambda b, t: (b, 0, 0)),
                  pl.BlockSpec((tq, 2 * n), lambda b, t: (t, 0)),
                  pl.BlockSpec((F_W, F_W), lambda b, t: (0, 0)),
                  pl.BlockSpec((F_W, F_W), lambda b, t: (0, 0))],
        out_specs=pl.BlockSpec((None, tq, F_W), lambda b, t: (b, t, 0)),
        out_shape=jax.ShapeDtypeStruct((B, n, F_W), BF),
        scratch_shapes=[pltpu.VMEM((2 * n, F_W), BF)],
        compiler_params=_cparams(2),
        name="fourier_mixer",
    )(fb_all, cs, cc, sc)


def _attend(q, kcat, vcat, add_bias, sinks):
    outs = []
    for h in range(N_KV_HEADS):
        qs = jnp.concatenate([q[:, (2 * h) * 128:(2 * h + 1) * 128],
                              q[:, (2 * h + 1) * 128:(2 * h + 2) * 128]], axis=0)
        s = add_bias(_dot_nt(qs, kcat), h)
        m = jnp.max(s, axis=-1, keepdims=True)
        if sinks is not None:
            row = lax.broadcasted_iota(jnp.int32, (2 * QB, 1), 0)
            sk = jnp.where(row < QB, sinks[2 * h], sinks[2 * h + 1])
            m = jnp.maximum(m, sk)
        p = jnp.exp(s - m)
        l = jnp.sum(p, axis=-1, keepdims=True)
        if sinks is not None:
            l = l + jnp.exp(sk - m)
        outs.append(_dot(p.astype(BF), vcat) / l)
    o0, o1 = outs
    lane = lax.broadcasted_iota(jnp.int32, (QB, 128), 1)
    lo = lane < HEAD_DIM
    left = jnp.where(lo, o0[:QB], pltpu.roll(o0[QB:], HEAD_DIM, axis=1))
    right = jnp.where(lo, pltpu.roll(o1[:QB], HEAD_DIM, axis=1), o1[QB:])
    return jnp.concatenate([left, right], axis=1)


def _attn_kernel(*refs, kind, with_ctx):
    if kind == "window":
        sink_ref, q_ref, k_ref, v_ref, o_ref = refs
        sinks = [sink_ref[i] for i in range(N_Q_HEADS)]
    elif kind == "na":
        q_ref, k_ref, v_ref, tab_ref, o_ref = refs
        sinks = None
    else:
        q_ref, k_ref, v_ref, o_ref = refs
        sinks = None
    j = pl.program_id(1)
    no_bias = lambda s, h: s

    def latent():
        q = q_ref[...]
        if kind == "global":
            o = _attend(q, k_ref[...], v_ref[...], no_bias, None)
        elif kind == "window":
            start = pl.multiple_of(QB * jnp.clip(j - 1, 0, N_LAT_QB - 3), QB)
            kcat = jnp.concatenate([k_ref[pl.ds(start, WIN_SPAN), :], k_ref[SEQ:N_TOK, :]], axis=0)
            vcat = jnp.concatenate([v_ref[pl.ds(start, WIN_SPAN), :], v_ref[SEQ:N_TOK, :]], axis=0)
            nk = WIN_SPAN + CTX_LEN
            col = lax.broadcasted_iota(jnp.int32, (2 * QB, nk), 1)
            row = lax.broadcasted_iota(jnp.int32, (2 * QB, nk), 0)
            q_pos = QB * j + row % QB
            k_pos = start + col
            ok = (col >= WIN_SPAN) | (jnp.abs(q_pos - k_pos) <= WINDOW)
            o = _attend(q, kcat, vcat, lambda s, h: jnp.where(ok, s, NEG_INF), sinks)
        else:
            kb = jnp.clip(j - 2, 0, N_LAT_QB - 5)
            start = pl.multiple_of(QB * kb, QB)
            kcat = jnp.concatenate([k_ref[pl.ds(start, NA_SPAN), :], k_ref[SEQ:N_TOK, :]], axis=0)
            vcat = jnp.concatenate([v_ref[pl.ds(start, NA_SPAN), :], v_ref[SEQ:N_TOK, :]], axis=0)
            tix = jnp.where(j < 2, j, jnp.where(j > N_LAT_QB - 3, j - (N_LAT_QB - 5), 2))

            def na_bias(s, h):
                b = tab_ref[tix, 2 * h:2 * h + 2]
                return s + b.reshape(2 * QB, NA_SPAN + CTX_LEN)

            o = _attend(q, kcat, vcat, na_bias, None)
        o_ref[...] = o.astype(BF)

    def context():
        o = _attend(q_ref[...], k_ref[SEQ:N_TOK, :], v_ref[SEQ:N_TOK, :], no_bias, sinks)
        o_ref[...] = o.astype(BF)

    if with_ctx:
        pl.when(j < N_LAT_QB)(latent)
        pl.when(j >= N_LAT_QB)(context)
    else:
        latent()


def _attention(kind, q, k, v, extra, with_ctx):
    B = q.shape[0]
    nblk = N_LAT_QB + (CTX_LEN // QB if with_ctx else 0)
    in_specs = [pl.BlockSpec((None, QB, 2 * Q_W), lambda b, j: (b, j, 0)),
                pl.BlockSpec((None, N_TOK, KV_W), lambda b, j: (b, 0, 0)),
                pl.BlockSpec((None, N_TOK, KV_W), lambda b, j: (b, 0, 0))]
    args = [q, k, v]
    if kind == "window":
        in_specs = [pl.BlockSpec(memory_space=pltpu.SMEM)] + in_specs
        args = [extra] + args
    elif kind == "na":
        in_specs = in_specs + [pl.BlockSpec(extra.shape, lambda b, j: (0, 0, 0, 0))]
        args = args + [extra]
    return pl.pallas_call(
        functools.partial(_attn_kernel, kind=kind, with_ctx=with_ctx),
        grid=(B, nblk),
        in_specs=in_specs,
        out_specs=pl.BlockSpec((None, QB, Q_W), lambda b, j: (b, j, 0)),
        out_shape=jax.ShapeDtypeStruct((B, N_TOK, Q_W), BF),
        compiler_params=_cparams(2),
        name="attention_" + kind,
    )(*args)


def _merge_kernel(*refs, with_ctx):
    if with_ctx:
        (h_ref, hc_ref, mod_ref, a_ref, bl_ref, bc_ref, c_ref, d_ref,
         wg_ref, wb_ref, wo_ref, g_ref, b_ref, o_ref) = refs
    else:
        (h_ref, mod_ref, a_ref, bl_ref, c_ref, d_ref,
         wg_ref, wb_ref, wo_ref, g_ref, b_ref, o_ref) = refs
    m = mod_ref[...]
    h = h_ref[...]
    br_b = bl_ref[...]
    if with_ctx:
        is_ctx = pl.program_id(1) == N_LAT_T
        h = jnp.where(is_ctx, hc_ref[...], h)
        br_b = jnp.where(is_ctx, bc_ref[...], br_b)
    u = (h * (1.0 + m[1:2]) + m[0:1]).astype(BF)
    branches = (a_ref[...], br_b, c_ref[...], d_ref[...])
    merged = None
    for i in range(N_BRANCH):
        gate = jax.nn.sigmoid(_dot(u, wg_ref[:, i * D_MODEL:(i + 1) * D_MODEL]))
        term = gate * _dot(branches[i], wb_ref[i])
        merged = term if merged is None else merged + term
    y = _dot(merged.astype(BF), wo_ref[...])
    o_ref[...] = _layer_norm(DN_ALPHA * h + m[2:3] * y, g_ref[...], b_ref[...])


def _merge(h_lat, h_ctx, mod, l, br_a, brb_lat, brb_ctx, br_c, br_d, w_gate, w_branch, w_out, ln_g, ln_b):
    B = h_lat.shape[0]
    with_ctx = brb_ctx is not None
    n_t = N_LAT_T + (1 if with_ctx else 0)
    tok = lambda w: pl.BlockSpec((None, TM, w), lambda b, t: (b, t, 0))
    layer = lambda shape: pl.BlockSpec((None,) + shape, lambda b, t: (l,) + (0,) * len(shape))
    if with_ctx:
        in_specs, args = _lat_ctx_specs(h_lat, h_ctx)
    else:
        in_specs, args = [tok(D_MODEL)], [h_lat]
    in_specs += [pl.BlockSpec((None, None, N_MOD, D_MODEL),
                              lambda b, t: (l, jnp.where(t == N_LAT_T, B, b), 0, 0)),
                 tok(Q_W),
                 pl.BlockSpec((None, TM, F_W), lambda b, t: (b, jnp.minimum(t, N_LAT_T - 1), 0))]
    args += [mod, br_a, brb_lat]
    if with_ctx:
        in_specs.append(pl.BlockSpec((None, TM, F_W), lambda b, t: (b, 0, 0)))
        args.append(brb_ctx)
    in_specs += [tok(Q_W), tok(Q_W), layer((D_MODEL, GATE_W)), layer((N_BRANCH, Q_W, D_MODEL)),
                 layer((D_MODEL, D_MODEL)), layer((1, D_MODEL)), layer((1, D_MODEL))]
    args += [br_c, br_d, w_gate, w_branch, w_out, ln_g, ln_b]
    return pl.pallas_call(
        functools.partial(_merge_kernel, with_ctx=with_ctx),
        grid=(B, n_t),
        in_specs=in_specs,
        out_specs=tok(D_MODEL),
        out_shape=jax.ShapeDtypeStruct((B, n_t * TM, D_MODEL), F32),
        compiler_params=_cparams(2),
        name="gated_merge",
    )(*args)


def _excl_cumsum_lanes(x, tri):
    n = x.shape[1]
    nc = n // 128
    xs = jnp.concatenate([x[:, c * 128:(c + 1) * 128] for c in range(nc)], axis=0)
    incl = _dot(xs.astype(BF), tri)
    excl = incl - xs
    tot = incl[:, 127:128]
    off = jnp.zeros((N_EXPERTS, 1), F32)
    pieces = []
    for c in range(nc):
        pieces.append(excl[c * N_EXPERTS:(c + 1) * N_EXPERTS] + off)
        off = off + tot[c * N_EXPERTS:(c + 1) * N_EXPERTS]
    return jnp.concatenate(pieces, axis=1)


MIN_NORMAL = 2.0 ** -126
MANTISSA_STEPS = 30


def _select_top(a, cap, tri):
    def count_ge(c):
        return jnp.sum(jnp.where(a >= c, 1.0, 0.0), axis=-1, keepdims=True)

    has = count_ge(MIN_NORMAL) >= cap
    p = jnp.full((N_EXPERTS, 1), MIN_NORMAL, F32)
    for s in (64, 32, 16, 8, 4, 2, 1):
        cand = p * (2.0 ** s)
        p = jnp.where(count_ge(cand) >= cap, cand, p)
    lo = jnp.where(has, p, 0.0)
    hi = jnp.where(has, 2.0 * p, MIN_NORMAL)
    for _ in range(MANTISSA_STEPS):
        mid = 0.5 * (lo + hi)
        ok = count_ge(mid) >= cap
        lo = jnp.where(ok, mid, lo)
        hi = jnp.where(ok, hi, mid)
    above = a >= hi
    edge = jnp.where(a >= lo, jnp.where(above, 0.0, 1.0), 0.0)
    need = cap - jnp.sum(jnp.where(above, 1.0, 0.0), axis=-1, keepdims=True)
    edge_rank = _excl_cumsum_lanes(edge, tri)
    sel = jnp.where(above, 1.0, jnp.where(edge_rank < need, edge, 0.0))
    return sel > 0.5, _excl_cumsum_lanes(sel, tri)


def _router_kernel(h_ref, mod_ref, wr_ref, tri_ref, u_ref, slot_em_ref, aff_em_ref, slot_tm_ref, *, cap):
    m = mod_ref[...]
    u = h_ref[...] * (1.0 + m[4:5]) + m[3:4]
    u_hi = u.astype(BF)
    u_ref[...] = u_hi
    n = u.shape[0]
    u_lo = (u - u_hi.astype(F32)).astype(BF)
    part = _dot(u_hi, wr_ref[...]) + _dot(u_lo, wr_ref[...])
    logits = part + pltpu.roll(part, R_LANES - N_EXPERTS, axis=1)
    lane = lax.broadcasted_iota(jnp.int32, logits.shape, 1)
    lg = jnp.where(lane < N_EXPERTS, logits, NEG_INF)
    mx = jnp.max(lg, axis=-1, keepdims=True)
    ex = jnp.exp(lg - mx)
    aff = ex / jnp.sum(ex, axis=-1, keepdims=True)
    aff_em = aff.T[0:N_EXPERTS]
    aff_em_ref[...] = aff_em
    sel, rank = _select_top(aff_em, cap, tri_ref[...])
    slot = jnp.where(sel, rank, -1.0)
    slot_em_ref[...] = slot.astype(jnp.int32)
    pad = jnp.full((R_LANES - N_EXPERTS, n), -1.0, F32)
    slot_tm_ref[...] = jnp.concatenate([slot, pad], axis=0).T.astype(jnp.int32)


def _router(h, mod, l, w_router_split, tri, *, n, blk, cap):
    B = h.shape[0]
    return pl.pallas_call(
        functools.partial(_router_kernel, cap=cap),
        grid=(B,),
        in_specs=[pl.BlockSpec((None, n, D_MODEL), lambda b: (b, blk, 0)),
                  pl.BlockSpec((None, None, N_MOD, D_MODEL),
                               lambda b: (l, b if n == SEQ else B, 0, 0)),
                  pl.BlockSpec((None, D_MODEL, R_LANES), lambda b: (l, 0, 0)),
                  pl.BlockSpec((128, 128), lambda b: (0, 0))],
        out_specs=[pl.BlockSpec((None, n, D_MODEL), lambda b: (b, 0, 0)),
                   pl.BlockSpec((None, N_EXPERTS, n), lambda b: (b, 0, 0)),
                   pl.BlockSpec((None, N_EXPERTS, n), lambda b: (b, 0, 0)),
                   pl.BlockSpec((None, n, R_LANES), lambda b: (b, 0, 0))],
        out_shape=[jax.ShapeDtypeStruct((B, n, D_MODEL), BF),
                   jax.ShapeDtypeStruct((B, N_EXPERTS, n), jnp.int32),
                   jax.ShapeDtypeStruct((B, N_EXPERTS, n), F32),
                   jax.ShapeDtypeStruct((B, n, R_LANES), jnp.int32)],
        compiler_params=_cparams(1),
        name="ec_router",
    )(h, mod, w_router_split, tri)


def _gather_rows(slot, aff, u, cap):
    n = slot.shape[1]
    hit = slot == lax.broadcasted_iota(jnp.int32, (cap, n), 0)
    onehot = jnp.where(hit, 1.0, 0.0).astype(BF)
    xs = _dot(onehot, u).astype(BF)
    w = jnp.sum(jnp.where(hit, aff, 0.0), axis=-1, keepdims=True)
    return xs, w


def _swiglu(xs, wg, wu, wd):
    a = _dot(xs, wg)
    up = _dot(xs, wu)
    hm = (a * jax.nn.sigmoid(a)) * up
    return _dot(hm.astype(BF), wd)


def _ffn_lat_kernel(u_ref, slot_ref, aff_ref, wg_ref, wu_ref, wd_ref, y_ref, wg_s, wu_s, wd_s):
    @pl.when(pl.program_id(1) == 0)
    def _():
        wg_s[...] = wg_ref[...].astype(BF)
        wu_s[...] = wu_ref[...].astype(BF)
        wd_s[...] = wd_ref[...].astype(BF)

    xs, w = _gather_rows(slot_ref[...], aff_ref[...], u_ref[...], CAP_LAT)
    y = _swiglu(xs, wg_s[...], wu_s[...], wd_s[...])
    y_ref[...] = (y * w).astype(BF)


def _ffn_ctx_kernel(u_ref, slot_ref, aff_ref, wg_ref, wu_ref, wd_ref, y_ref):
    B = u_ref.shape[0]
    xs_l, w_l = [], []
    for b in range(B):
        xs, w = _gather_rows(slot_ref[b], aff_ref[b], u_ref[b], CAP_CTX)
        xs_l.append(xs)
        w_l.append(w)
    y = _swiglu(jnp.concatenate(xs_l, axis=0), wg_ref[...].astype(BF), wu_ref[...].astype(BF),
                wd_ref[...].astype(BF))
    y = y * jnp.concatenate(w_l, axis=0)
    for b in range(B):
        y_ref[b] = y[b * CAP_CTX:(b + 1) * CAP_CTX].astype(BF)


def _ffn_lat(u, slot_em, aff_em, l, wg, wu, wd):
    B = u.shape[0]
    F = wg.shape[-1]
    row = lambda: pl.BlockSpec((None, None, 1, SEQ), lambda e, b: (b, e, 0, 0))
    wspec = lambda s: pl.BlockSpec((None, None) + s, lambda e, b: (l, e, 0, 0))
    return pl.pallas_call(
        _ffn_lat_kernel,
        grid=(N_EXPERTS, B),
        in_specs=[pl.BlockSpec((None, SEQ, D_MODEL), lambda e, b: (b, 0, 0)), row(), row(),
                  wspec((D_MODEL, F)), wspec((D_MODEL, F)), wspec((F, D_MODEL))],
        out_specs=pl.BlockSpec((None, CAP_LAT, D_MODEL), lambda e, b: (b, e, 0)),
        out_shape=jax.ShapeDtypeStruct((B, N_EXPERTS * CAP_LAT, D_MODEL), BF),
        scratch_shapes=[pltpu.VMEM((D_MODEL, F), BF), pltpu.VMEM((D_MODEL, F), BF),
                        pltpu.VMEM((F, D_MODEL), BF)],
        compiler_params=_cparams(2),
        name="ec_ffn_latent",
    )(u, slot_em, aff_em, wg, wu, wd)


def _ffn_ctx(u, slot_em, aff_em, l, wg, wu, wd):
    B = u.shape[0]
    F = wg.shape[-1]
    row = lambda: pl.BlockSpec((B, None, 1, CTX_LEN), lambda e: (0, e, 0, 0))
    wspec = lambda s: pl.BlockSpec((None, None) + s, lambda e: (l, e, 0, 0))
    return pl.pallas_call(
        _ffn_ctx_kernel,
        grid=(N_EXPERTS,),
        in_specs=[pl.BlockSpec((B, CTX_LEN, D_MODEL), lambda e: (0, 0, 0),
                               pipeline_mode=pl.Buffered(1)), row(), row(),
                  wspec((D_MODEL, F)), wspec((D_MODEL, F)), wspec((F, D_MODEL))],
        out_specs=pl.BlockSpec((B, CAP_CTX, D_MODEL), lambda e: (0, e, 0)),
        out_shape=jax.ShapeDtypeStruct((B, N_EXPERTS * CAP_CTX, D_MODEL), BF),
        compiler_params=_cparams(1),
        name="ec_ffn_context",
    )(u, slot_em, aff_em, wg, wu, wd)


def _scatter_matrix(slot_tm, cap):
    rows = slot_tm.shape[0]
    if cap % 128 == 0:
        lane = lax.broadcasted_iota(jnp.int32, (rows, cap), 1)
        pieces = [jnp.where(slot_tm[:, e:e + 1] == lane, 1.0, 0.0).astype(BF)
                  for e in range(N_EXPERTS)]
        return jnp.concatenate(pieces, axis=1)
    lane = lax.broadcasted_iota(jnp.int32, (rows, N_EXPERTS * cap), 1)
    acc = jnp.zeros((rows, N_EXPERTS * cap), F32)
    for e in range(N_EXPERTS):
        s = slot_tm[:, e:e + 1]
        target = jnp.where(s >= 0, s + e * cap, -1)
        acc = acc + jnp.where(target == lane, 1.0, 0.0)
    return acc.astype(BF)


def _scatter_kernel(*refs, with_ctx):
    if with_ctx:
        h_ref, mod_ref, sl_ref, yl_ref, sc_ref, yc_ref, g_ref, b_ref, o_ref = refs
    else:
        h_ref, mod_ref, sl_ref, yl_ref, g_ref, b_ref, o_ref = refs

    def finish(f):
        m = mod_ref[...]
        o_ref[...] = _layer_norm(DN_ALPHA * h_ref[...] + m[5:6] * f, g_ref[...], b_ref[...])

    def latent():
        finish(_dot(_scatter_matrix(sl_ref[...], CAP_LAT), yl_ref[...]))

    def context():
        finish(_dot(_scatter_matrix(sc_ref[...], CAP_CTX), yc_ref[...]))

    if with_ctx:
        t = pl.program_id(1)
        pl.when(t < N_LAT_T)(latent)
        pl.when(t == N_LAT_T)(context)
    else:
        latent()


def _scatter(h1, mod, l, slot_lat, y_lat, slot_ctx, y_ctx, ln_g, ln_b):
    B = h1.shape[0]
    with_ctx = slot_ctx is not None
    n_t = N_LAT_T + (1 if with_ctx else 0)
    tok = lambda w: pl.BlockSpec((None, TM, w), lambda b, t: (b, t, 0))
    in_specs = [tok(D_MODEL),
                pl.BlockSpec((None, None, N_MOD, D_MODEL),
                             lambda b, t: (l, jnp.where(t == N_LAT_T, B, b), 0, 0)),
                pl.BlockSpec((None, TM, R_LANES), lambda b, t: (b, jnp.minimum(t, N_LAT_T - 1), 0)),
                pl.BlockSpec((None, N_EXPERTS * CAP_LAT, D_MODEL), lambda b, t: (b, 0, 0))]
    args = [h1, mod, slot_lat, y_lat]
    if with_ctx:
        in_specs += [pl.BlockSpec((None, TM, R_LANES), lambda b, t: (b, 0, 0)),
                     pl.BlockSpec((None, N_EXPERTS * CAP_CTX, D_MODEL), lambda b, t: (b, 0, 0))]
        args += [slot_ctx, y_ctx]
    in_specs += [pl.BlockSpec((None, 1, D_MODEL), lambda b, t: (l, 0, 0)),
                 pl.BlockSpec((None, 1, D_MODEL), lambda b, t: (l, 0, 0))]
    args += [ln_g, ln_b]
    return pl.pallas_call(
        functools.partial(_scatter_kernel, with_ctx=with_ctx),
        grid=(B, n_t),
        in_specs=in_specs,
        out_specs=tok(D_MODEL),
        out_shape=jax.ShapeDtypeStruct((B, n_t * TM, D_MODEL), F32),
        compiler_params=_cparams(2),
        name="ec_scatter_norm",
    )(*args)


def _rope_tables():
    t = jnp.arange(SEQ)
    axis_dim = HEAD_DIM // 2
    inv = ROPE_THETA ** (-jnp.arange(0, axis_dim, 2, dtype=F32) / axis_dim)
    ang_r = (t // GRID_W).astype(F32)[:, None] * inv
    ang_c = (t % GRID_W).astype(F32)[:, None] * inv
    cr, sr, cc, sc = jnp.cos(ang_r), jnp.sin(ang_r), jnp.cos(ang_c), jnp.sin(ang_c)
    cos_h = jnp.concatenate([cr, cr, cc, cc], axis=-1)
    sin_h = jnp.concatenate([-sr, sr, -sc, sc], axis=-1)
    cos_t = jnp.concatenate([jnp.tile(cos_h, (1, N_Q_HEADS)), jnp.ones((CTX_LEN, Q_W), F32)], axis=0)
    sin_t = jnp.concatenate([jnp.tile(sin_h, (1, N_Q_HEADS)), jnp.zeros((CTX_LEN, Q_W), F32)], axis=0)
    return cos_t, sin_t


def _dft_tables(n):
    k = np.arange(n, dtype=np.int64)
    ang = 2.0 * np.pi * ((k[:, None] * k[None, :]) % n).astype(np.float64) / n
    cs = np.concatenate([np.cos(ang), np.sin(ang)], axis=1).astype(np.float32)
    c = np.arange(F_W)
    same = (c[:, None] // FOURIER_GROUP_W) == (c[None, :] // FOURIER_GROUP_W)
    angc = 2.0 * np.pi * (((c[:, None] % FOURIER_GROUP_W) * (c[None, :] % FOURIER_GROUP_W))
                          % FOURIER_GROUP_W) / FOURIER_GROUP_W
    cc = np.where(same, np.cos(angc), 0.0).astype(np.float32)
    sc = np.where(same, np.sin(angc), 0.0).astype(np.float32)
    return cs, cc, sc


def _na_bias_table(rpb):
    rows = SEQ // GRID_W
    q_rows, k_rows = QB // GRID_W, NA_SPAN // GRID_W
    n_dc = GRID_W - NA_KW
    padded = jnp.pad(rpb.astype(F32), ((0, 0), (0, 0), (n_dc, n_dc)))
    toe = jnp.stack([padded[:, :, GRID_W - 1 - c:2 * GRID_W - 1 - c] for c in range(GRID_W)], axis=2)
    zero_blk = jnp.zeros((N_Q_HEADS, GRID_W, GRID_W), F32)
    col = np.arange(GRID_W)
    c0 = np.clip(col - NA_KW // 2, 0, GRID_W - NA_KW)
    col_ok = (col[None, :] >= c0[:, None]) & (col[None, :] < c0[:, None] + NA_KW)
    tables = []
    for j in (0, 1, 2, N_LAT_QB - 2, N_LAT_QB - 1):
        kb = min(max(j - 2, 0), N_LAT_QB - 5)
        q_parts = []
        for qr in range(q_rows):
            r = q_rows * j + qr
            r0 = min(max(r - NA_KH // 2, 0), rows - NA_KH)
            k_parts = []
            for ki in range(k_rows):
                kr = q_rows * kb + ki
                if r0 <= kr < r0 + NA_KH:
                    k_parts.append(jnp.where(col_ok[None], toe[:, kr - r + NA_KH - 1], NEG_INF))
                else:
                    k_parts.append(zero_blk + NEG_INF)
            q_parts.append(jnp.concatenate(k_parts, axis=-1))
        tables.append(jnp.concatenate(q_parts, axis=-2))
    tab = jnp.stack(tables, axis=0)
    return jnp.concatenate([tab, jnp.zeros(tab.shape[:3] + (CTX_LEN,), F32)], axis=-1)


def kernel(x, c, ctx, c_ctx, w_mod, b_mod, w_in, qk_gain, sink_logit, na_rpb, w_branch, w_out,
           ln1_g, ln1_b, w_router, w_gate, w_up, w_down, ln2_g, ln2_b):
    B = x.shape[0]
    L = w_mod.shape[0]
    assert x.shape[1:] == (SEQ, D_MODEL) and ctx.shape[1:] == (CTX_LEN, D_MODEL)

    n_rows = -(-(B + 1) // 8) * 8
    c_all = jnp.concatenate([c, c_ctx[None, :], jnp.zeros((n_rows - B - 1, D_MODEL), F32)], axis=0)
    mod_all = _modulation(c_all, w_mod, b_mod).reshape(L, n_rows, N_MOD, D_MODEL)

    cos_t, sin_t = _rope_tables()
    gidx = np.arange(128)
    bd = jnp.asarray(np.where((gidx[:, None] // HEAD_DIM) == (gidx[None, :] // HEAD_DIM),
                              1.0 / HEAD_DIM, 0.0), BF)
    tri = jnp.asarray(gidx[:, None] <= gidx[None, :], BF)
    cs_lat, cc, sc = _dft_tables(SEQ)
    cs_lat, cc, sc = jnp.asarray(cs_lat).astype(BF), jnp.asarray(cc).astype(BF), jnp.asarray(sc).astype(BF)
    cs_ctx = jnp.asarray(_dft_tables(CTX_LEN)[0]).astype(BF)

    w_mix_bf = w_in[:, :, :MIX_W].astype(BF)
    w_gatecols_bf = w_in[:, :, MIX_W:].astype(BF)
    w_branch_bf = w_branch.astype(BF)
    w_out_bf = w_out.astype(BF)
    wr_hi = w_router.astype(BF)
    wr_lo = (w_router - wr_hi.astype(F32)).astype(BF)
    wr_split = jnp.concatenate(
        [wr_hi, wr_lo, jnp.zeros((L, D_MODEL, R_LANES - 2 * N_EXPERTS), BF)], axis=-1)
    ln1_g, ln1_b, ln2_g, ln2_b = (a[:, None, :] for a in (ln1_g, ln1_b, ln2_g, ln2_b))
    per_expert = lambda a: a.reshape(a.shape[0], N_EXPERTS, 1, a.shape[-1])

    h_lat, h_ctx = x, ctx
    for l in range(L):
        with_ctx = l < L - 1
        gq = jnp.tile(qk_gain[l, 0], 2)[None, :]
        gk = jnp.tile(qk_gain[l, 1], 2)[None, :]
        qa, ka, va, fb, qc, kc, vc, qd, kd, vd = _project(
            h_lat, h_ctx, mod_all, l, w_mix_bf, gq, gk, cos_t, sin_t, bd)

        br_a = _attention("global", qa, ka, va, None, with_ctx)
        br_c = _attention("window", qc, kc, vc, sink_logit[l], with_ctx)
        br_d = _attention("na", qd, kd, vd, _na_bias_table(na_rpb[l]), with_ctx)
        brb_lat = _fourier(fb, cs_lat, cc, sc, row0=0, n=SEQ, tq=512)
        brb_ctx = _fourier(fb, cs_ctx, cc, sc, row0=SEQ, n=CTX_LEN, tq=CTX_LEN) if with_ctx else None

        h1 = _merge(h_lat, h_ctx, mod_all, l, br_a, brb_lat, brb_ctx, br_c, br_d,
                    w_gatecols_bf, w_branch_bf, w_out_bf, ln1_g, ln1_b)

        u_lat, slot_em, aff_em, slot_tm = _router(h1, mod_all, l, wr_split, tri,
                                                  n=SEQ, blk=0, cap=CAP_LAT)
        y_lat = _ffn_lat(u_lat, per_expert(slot_em), per_expert(aff_em), l, w_gate, w_up, w_down)
        if with_ctx:
            u_c, slot_em_c, aff_em_c, slot_tm_c = _router(
                h1, mod_all, l, wr_split, tri, n=CTX_LEN, blk=SEQ // CTX_LEN, cap=CAP_CTX)
            y_ctx = _ffn_ctx(u_c, per_expert(slot_em_c), per_expert(aff_em_c), l, w_gate, w_up, w_down)
        else:
            slot_tm_c, y_ctx = None, None
        h_lat = _scatter(h1, mod_all, l, slot_tm, y_lat, slot_tm_c, y_ctx, ln2_g, ln2_b)
        h_ctx = None
    return h_lat
```

```python
import functools

import numpy as np
import jax
import jax.numpy as jnp
from jax import lax
from jax.experimental import pallas as pl
from jax.experimental.pallas import tpu as pltpu

D_MODEL = 1024
SEQ = 2048
DEPTH = 2
CTX_LEN = 256
GRID_W = 64
HEAD_DIM = 64
N_Q_HEADS = 4
N_KV_HEADS = 2
Q_W = 256
KV_W = 128
F_W = 256
FOURIER_GROUP_W = 64
N_BRANCH = 4
MIX_W = 3 * (Q_W + 2 * KV_W) + F_W
GATE_W = N_BRANCH * D_MODEL
WINDOW = 128
NA_KH = 8
NA_KW = 16
ROPE_THETA = 10000.0
N_EXPERTS = 16
EC_CAPACITY = 2
N_MOD = 6
DN_ALPHA = (2 * DEPTH) ** 0.25
LN_EPS = 1e-6
RMS_EPS = 1e-6
NEG_INF = -1e30

N_TOK = SEQ + CTX_LEN
TM = 256
QB = 128
NQ = 2
MIXER_KINDS = ("global", "window", "na")
TM_SCATTER = 512
N_LAT_T = SEQ // TM
N_LAT_QB = SEQ // QB
WIN_SPAN = QB + 2 * WINDOW
NA_SPAN = 640
NA_TYPES = 5
CAP_LAT = EC_CAPACITY * SEQ // N_EXPERTS
CAP_CTX = EC_CAPACITY * CTX_LEN // N_EXPERTS
R_LANES = 128

BF = jnp.bfloat16
F32 = jnp.float32
VMEM_LIMIT = 56 * 1024 * 1024


def _cparams(n_axes):
    return pltpu.CompilerParams(dimension_semantics=("arbitrary",) * n_axes,
                                vmem_limit_bytes=VMEM_LIMIT)


def _dot(a, b):
    return jnp.dot(a, b, preferred_element_type=F32)


def _dot_nt(a, b):
    return lax.dot_general(a, b, (((1,), (1,)), ((), ())), preferred_element_type=F32)


def _layer_norm(x, g, b):
    mu = jnp.mean(x, axis=-1, keepdims=True)
    xc = x - mu
    var = jnp.mean(xc * xc, axis=-1, keepdims=True)
    return xc * lax.rsqrt(var + LN_EPS) * g + b


def _mod_kernel(c_ref, w_ref, b_ref, o_ref):
    c = c_ref[...]
    a = c * jax.nn.sigmoid(c)
    o_ref[...] = jnp.dot(a, w_ref[...], precision=lax.Precision.HIGHEST,
                         preferred_element_type=F32) + b_ref[...]


def _modulation(c_all, w_mod, b_mod):
    L = w_mod.shape[0]
    R = c_all.shape[0]
    ncol = N_MOD * D_MODEL
    tn = D_MODEL
    return pl.pallas_call(
        _mod_kernel,
        grid=(L, ncol // tn),
        in_specs=[pl.BlockSpec((R, D_MODEL), lambda l, j: (0, 0)),
                  pl.BlockSpec((None, D_MODEL, tn), lambda l, j: (l, 0, j)),
                  pl.BlockSpec((None, 1, tn), lambda l, j: (l, 0, j))],
        out_specs=pl.BlockSpec((None, R, tn), lambda l, j: (l, 0, j)),
        out_shape=jax.ShapeDtypeStruct((L, R, ncol), F32),
        compiler_params=_cparams(2),
        name="modulation",
    )(c_all, w_mod, b_mod.reshape(L, 1, ncol))


def _rope128(x, cos, sin):
    lane = lax.broadcasted_iota(jnp.int32, x.shape, 1)
    first = (lane % 32) < 16
    up = pltpu.roll(x, 128 - 16, axis=1)
    dn = pltpu.roll(x, 16, axis=1)
    return x * cos + jnp.where(first, up, dn) * sin


def _rms128(x, gain, bd):
    x2 = x * x
    hi = x2.astype(BF)
    lo = (x2 - hi.astype(F32)).astype(BF)
    ms = _dot(hi, bd) + _dot(lo, bd)
    return x * lax.rsqrt(ms + RMS_EPS) * gain


def _q_slabs(x0, x1):
    lane = lax.broadcasted_iota(jnp.int32, x0.shape, 1)
    lo = lane < HEAD_DIM
    x0r = pltpu.roll(x0, HEAD_DIM, axis=1)
    x1r = pltpu.roll(x1, HEAD_DIM, axis=1)
    scale = HEAD_DIM ** -0.5
    slabs = [jnp.where(lo, x0, 0.0), jnp.where(lo, x0r, 0.0),
             jnp.where(lo, 0.0, x1r), jnp.where(lo, 0.0, x1)]
    return jnp.concatenate([(s * scale).astype(BF) for s in slabs], axis=1)


def _proj_kernel(hl_ref, hc_ref, mod_ref, w_ref, gq_ref, gk_ref, cos_ref, sin_ref, bd_ref,
                 q_ref, kv_ref):
    m = mod_ref[...]
    h = jnp.where(pl.program_id(1) == N_LAT_T, hc_ref[...], hl_ref[...])
    u = h * (1.0 + m[1:2]) + m[0:1]
    p = _dot(u.astype(BF), w_ref[...])
    cos = cos_ref[...]
    sin = sin_ref[...]
    bd = bd_ref[...]
    gq = gq_ref[...]
    gk = gk_ref[...]
    c0, s0 = cos[:, :128], sin[:, :128]
    c1, s1 = cos[:, 128:], sin[:, 128:]

    qa0 = _rope128(_rms128(p[:, 0:128], gq, bd), c0, s0)
    qa1 = _rope128(_rms128(p[:, 128:256], gq, bd), c1, s1)
    q_ref[:, 0:512] = _q_slabs(qa0, qa1)
    q_ref[:, 512:1024] = _q_slabs(_rope128(p[:, 768:896], c0, s0), _rope128(p[:, 896:1024], c1, s1))
    q_ref[:, 1024:1536] = _q_slabs(p[:, 1280:1408], p[:, 1408:1536])
    kv_ref[:, 0:128] = _rope128(_rms128(p[:, 256:384], gk, bd), c0, s0).astype(BF)
    kv_ref[:, 128:256] = p[:, 384:512].astype(BF)
    kv_ref[:, 256:384] = _rope128(p[:, 1024:1152], c0, s0).astype(BF)
    kv_ref[:, 384:512] = p[:, 1152:1280].astype(BF)
    kv_ref[:, 512:768] = p[:, 1536:1792].astype(BF)
    kv_ref[:, 768:1024] = p[:, 512:768].astype(BF)


QS_W = 2 * Q_W
KV_COL = {"global": (0, 1), "window": (2, 3), "na": (4, 5)}
Q_COL = {"global": 0, "window": 1, "na": 2}
ACT_W = 1024
FB_COL = 3


def _lat_ctx_specs(h_lat, h_ctx):
    return ([pl.BlockSpec((None, TM, D_MODEL), lambda b, t: (b, jnp.minimum(t, N_LAT_T - 1), 0)),
             pl.BlockSpec((None, TM, D_MODEL), lambda b, t: (b, 0, 0))], [h_lat, h_ctx])


def _project(h_lat, h_ctx, mod, l, w_mix, gq, gk, cos_t, sin_t, bd):
    B = h_lat.shape[0]
    n_t = N_TOK // TM
    ctx_row = B
    tok = lambda w: pl.BlockSpec((None, TM, w), lambda b, t: (b, t, 0))
    const = lambda shape: pl.BlockSpec(shape, lambda b, t: (0,) * len(shape))
    widths = (3 * QS_W, ACT_W)
    h_specs, h_args = _lat_ctx_specs(h_lat, h_ctx)
    return pl.pallas_call(
        _proj_kernel,
        grid=(B, n_t),
        in_specs=h_specs + [
            pl.BlockSpec((None, None, N_MOD, D_MODEL),
                         lambda b, t: (l, jnp.where(t == N_LAT_T, ctx_row, b), 0, 0)),
            pl.BlockSpec((None, D_MODEL, MIX_W), lambda b, t: (l, 0, 0)),
            const((1, 128)), const((1, 128)),
            pl.BlockSpec((TM, Q_W), lambda b, t: (t, 0)),
            pl.BlockSpec((TM, Q_W), lambda b, t: (t, 0)),
            const((128, 128))],
        out_specs=[tok(w) for w in widths],
        out_shape=[jax.ShapeDtypeStruct((B, N_TOK, w), BF) for w in widths],
        compiler_params=_cparams(2),
        name="mixer_projection",
    )(*h_args, mod, w_mix, gq, gk, cos_t, sin_t, bd)


def _fourier_kernel(x_ref, cs_ref, cc_ref, sc_ref, o_ref, z_ref, *, row0, n):
    @pl.when(pl.program_id(1) == 0)
    def _():
        x = x_ref[row0:row0 + n, :]
        z_ref[0:n, :] = _dot(x, cc_ref[...]).astype(BF)
        z_ref[n:2 * n, :] = (-_dot(x, sc_ref[...])).astype(BF)

    scale = (n * FOURIER_GROUP_W) ** -0.5
    o_ref[...] = (_dot(cs_ref[...], z_ref[...]) * scale).astype(BF)


def _fourier(fb_all, cs, cc, sc, *, row0, n, tq):
    B = fb_all.shape[0]
    return pl.pallas_call(
        functools.partial(_fourier_kernel, row0=row0, n=n),
        grid=(B, n // tq),
        in_specs=[pl.BlockSpec((None, N_TOK, F_W), lambda b, t: (b, 0, FB_COL)),
                  pl.BlockSpec((tq, 2 * n), lambda b, t: (t, 0)),
                  pl.BlockSpec((F_W, F_W), lambda b, t: (0, 0)),
                  pl.BlockSpec((F_W, F_W), lambda b, t: (0, 0))],
        out_specs=pl.BlockSpec((None, tq, F_W), lambda b, t: (b, t, 0)),
        out_shape=jax.ShapeDtypeStruct((B, n, F_W), BF),
        scratch_shapes=[pltpu.VMEM((2 * n, F_W), BF)],
        compiler_params=_cparams(2),
        name="fourier_mixer",
    )(fb_all, cs, cc, sc)


def _attend(q, kcat, vcat, add_bias, sinks):
    outs = []
    for h in range(N_KV_HEADS):
        qs = jnp.concatenate([q[:, (2 * h) * 128:(2 * h + 1) * 128],
                              q[:, (2 * h + 1) * 128:(2 * h + 2) * 128]], axis=0)
        s = add_bias(_dot_nt(qs, kcat), h)
        m = jnp.max(s, axis=-1, keepdims=True)
        if sinks is not None:
            row = lax.broadcasted_iota(jnp.int32, (2 * QB, 1), 0)
            sk = jnp.where(row < QB, sinks[2 * h], sinks[2 * h + 1])
            m = jnp.maximum(m, sk)
        p = jnp.exp(s - m)
        l = jnp.sum(p, axis=-1, keepdims=True)
        if sinks is not None:
            l = l + jnp.exp(sk - m)
        outs.append(_dot(p.astype(BF), vcat) / l)
    o0, o1 = outs
    lane = lax.broadcasted_iota(jnp.int32, (QB, 128), 1)
    lo = lane < HEAD_DIM
    left = jnp.where(lo, o0[:QB], pltpu.roll(o0[QB:], HEAD_DIM, axis=1))
    right = jnp.where(lo, pltpu.roll(o1[:QB], HEAD_DIM, axis=1), o1[QB:])
    return jnp.concatenate([left, right], axis=1)


def _mixers_kernel(*refs, kinds, with_ctx):
    refs = list(refs)
    sinks = None
    if "window" in kinds:
        sink_ref = refs.pop(0)
        sinks = [sink_ref[i] for i in range(N_Q_HEADS)]
    io = {}
    for kind in kinds:
        io[kind] = [refs.pop(0), refs.pop(0), refs.pop(0)]
    if "na" in kinds:
        tab_ref = refs.pop(0)
    out_ref = refs.pop(0)
    for n, kind in enumerate(kinds):
        io[kind].append(out_ref.at[:, n * Q_W:(n + 1) * Q_W])
    step = pl.program_id(1)
    no_bias = lambda s, h: s

    def lat_block(kind, j, q, k_ref, v_ref):
        ctx_k, ctx_v = k_ref[SEQ:N_TOK, :], v_ref[SEQ:N_TOK, :]
        if kind == "global":
            return _attend(q, k_ref[...], v_ref[...], no_bias, None)
        if kind == "window":
            start = pl.multiple_of(QB * jnp.clip(j - 1, 0, N_LAT_QB - 3), QB)
            kcat = jnp.concatenate([k_ref[pl.ds(start, WIN_SPAN), :], ctx_k], axis=0)
            vcat = jnp.concatenate([v_ref[pl.ds(start, WIN_SPAN), :], ctx_v], axis=0)
            nk = WIN_SPAN + CTX_LEN
            col = lax.broadcasted_iota(jnp.int32, (2 * QB, nk), 1)
            row = lax.broadcasted_iota(jnp.int32, (2 * QB, nk), 0)
            q_pos = QB * j + row % QB
            ok = (col >= WIN_SPAN) | (jnp.abs(q_pos - (start + col)) <= WINDOW)
            return _attend(q, kcat, vcat, lambda s, h: jnp.where(ok, s, NEG_INF), sinks)
        start = pl.multiple_of(QB * jnp.clip(j - 2, 0, N_LAT_QB - 5), QB)
        kcat = jnp.concatenate([k_ref[pl.ds(start, NA_SPAN), :], ctx_k], axis=0)
        vcat = jnp.concatenate([v_ref[pl.ds(start, NA_SPAN), :], ctx_v], axis=0)
        tix = jnp.where(j < 2, j, jnp.where(j > N_LAT_QB - 3, j - (N_LAT_QB - 5), 2))

        def na_bias(s, h):
            b = tab_ref[tix, 2 * h:2 * h + 2]
            return s + b.reshape(2 * QB, NA_SPAN + CTX_LEN)

        return _attend(q, kcat, vcat, na_bias, None)

    def latent():
        for i in range(NQ):
            rows = slice(i * QB, (i + 1) * QB)
            for kind in kinds:
                q_ref, k_ref, v_ref, o_ref = io[kind]
                o_ref[rows, :] = lat_block(kind, step * NQ + i, q_ref[rows, :], k_ref, v_ref).astype(BF)

    def context():
        for i in range(NQ):
            rows = slice(i * QB, (i + 1) * QB)
            for kind in kinds:
                q_ref, k_ref, v_ref, o_ref = io[kind]
                o = _attend(q_ref[rows, :], k_ref[SEQ:N_TOK, :], v_ref[SEQ:N_TOK, :], no_bias,
                            sinks if kind == "window" else None)
                o_ref[rows, :] = o.astype(BF)

    if with_ctx:
        pl.when(step < N_LAT_QB // NQ)(latent)
        pl.when(step >= N_LAT_QB // NQ)(context)
    else:
        latent()


def _mixers(kinds, q_all, kv_all, sink, na_table, with_ctx):
    B = q_all.shape[0]
    n_steps = (N_LAT_QB + (CTX_LEN // QB if with_ctx else 0)) // NQ
    in_specs, args = [], []
    if "window" in kinds:
        in_specs.append(pl.BlockSpec(memory_space=pltpu.SMEM))
        args.append(sink)
    for kind in kinds:
        qc, (kc, vc) = Q_COL[kind], KV_COL[kind]
        in_specs += [pl.BlockSpec((None, NQ * QB, QS_W), lambda b, j, c=qc: (b, j, c)),
                     pl.BlockSpec((None, N_TOK, KV_W), lambda b, j, c=kc: (b, 0, c)),
                     pl.BlockSpec((None, N_TOK, KV_W), lambda b, j, c=vc: (b, 0, c))]
        args += [q_all, kv_all, kv_all]
    if "na" in kinds:
        in_specs.append(pl.BlockSpec(na_table.shape, lambda b, j: (0, 0, 0, 0)))
        args.append(na_table)
    width = Q_W * len(kinds)
    return pl.pallas_call(
        functools.partial(_mixers_kernel, kinds=kinds, with_ctx=with_ctx),
        grid=(B, n_steps),
        in_specs=in_specs,
        out_specs=pl.BlockSpec((None, NQ * QB, width), lambda b, j: (b, j, 0)),
        out_shape=jax.ShapeDtypeStruct((B, n_steps * NQ * QB, width), BF),
        compiler_params=_cparams(2),
        name="attention_" + "_".join(kinds),
    )(*args)


def _merge_kernel(*refs, with_ctx):
    if with_ctx:
        (h_ref, hc_ref, mod_ref, acd_ref, bl_ref, bc_ref,
         wg_ref, wb_ref, wo_ref, g_ref, b_ref, wr_ref, h1_ref, u_ref, aff_ref) = refs
    else:
        (h_ref, mod_ref, acd_ref, bl_ref,
         wg_ref, wb_ref, wo_ref, g_ref, b_ref, wr_ref, h1_ref, u_ref, aff_ref) = refs
    m = mod_ref[...]
    h = h_ref[...]
    br_b = bl_ref[...]
    if with_ctx:
        is_ctx = pl.program_id(1) == N_LAT_T
        h = jnp.where(is_ctx, hc_ref[...], h)
        br_b = jnp.where(is_ctx, bc_ref[...], br_b)
    u = (h * (1.0 + m[1:2]) + m[0:1]).astype(BF)
    acd = acd_ref[...]
    branches = (acd[:, 0:Q_W], br_b, acd[:, Q_W:2 * Q_W], acd[:, 2 * Q_W:3 * Q_W])
    merged = None
    for i in range(N_BRANCH):
        gate = jax.nn.sigmoid(_dot(u, wg_ref[:, i * D_MODEL:(i + 1) * D_MODEL]))
        term = gate * _dot(branches[i], wb_ref[i])
        merged = term if merged is None else merged + term
    y = _dot(merged.astype(BF), wo_ref[...])
    h1 = _layer_norm(DN_ALPHA * h + m[2:3] * y, g_ref[...], b_ref[...])
    h1_ref[...] = h1

    u2 = h1 * (1.0 + m[4:5]) + m[3:4]
    u_hi = u2.astype(BF)
    u_ref[...] = u_hi
    u_lo = (u2 - u_hi.astype(F32)).astype(BF)
    part = _dot(u_hi, wr_ref[...]) + _dot(u_lo, wr_ref[...])
    logits = part + pltpu.roll(part, R_LANES - N_EXPERTS, axis=1)
    lane = lax.broadcasted_iota(jnp.int32, logits.shape, 1)
    lg = jnp.where(lane < N_EXPERTS, logits, NEG_INF)
    ex = jnp.exp(lg - jnp.max(lg, axis=-1, keepdims=True))
    aff = ex / jnp.sum(ex, axis=-1, keepdims=True)
    aff_ref[...] = aff.T[0:N_EXPERTS]


def _merge(h_lat, h_ctx, mod, l, br_acd, brb_lat, brb_ctx, w_gate, w_branch, w_out, ln_g, ln_b, w_router_split):
    B = h_lat.shape[0]
    with_ctx = brb_ctx is not None
    n_t = N_LAT_T + (1 if with_ctx else 0)
    tok = lambda w: pl.BlockSpec((None, TM, w), lambda b, t: (b, t, 0))
    layer = lambda shape: pl.BlockSpec((None,) + shape, lambda b, t: (l,) + (0,) * len(shape))
    if with_ctx:
        in_specs, args = _lat_ctx_specs(h_lat, h_ctx)
    else:
        in_specs, args = [tok(D_MODEL)], [h_lat]
    in_specs += [pl.BlockSpec((None, None, N_MOD, D_MODEL),
                              lambda b, t: (l, jnp.where(t == N_LAT_T, B, b), 0, 0)),
                 tok(3 * Q_W),
                 pl.BlockSpec((None, TM, F_W), lambda b, t: (b, jnp.minimum(t, N_LAT_T - 1), 0))]
    args += [mod, br_acd, brb_lat]
    if with_ctx:
        in_specs.append(pl.BlockSpec((None, TM, F_W), lambda b, t: (b, 0, 0)))
        args.append(brb_ctx)
    in_specs += [layer((D_MODEL, GATE_W)), layer((N_BRANCH, Q_W, D_MODEL)),
                 layer((D_MODEL, D_MODEL)), layer((1, D_MODEL)), layer((1, D_MODEL)),
                 layer((D_MODEL, R_LANES))]
    args += [w_gate, w_branch, w_out, ln_g, ln_b, w_router_split]
    rows = n_t * TM
    return pl.pallas_call(
        functools.partial(_merge_kernel, with_ctx=with_ctx),
        grid=(B, n_t),
        in_specs=in_specs,
        out_specs=[tok(D_MODEL), tok(D_MODEL),
                   pl.BlockSpec((None, N_EXPERTS, TM), lambda b, t: (b, 0, t))],
        out_shape=[jax.ShapeDtypeStruct((B, rows, D_MODEL), F32),
                   jax.ShapeDtypeStruct((B, rows, D_MODEL), BF),
                   jax.ShapeDtypeStruct((B, N_EXPERTS, rows), F32)],
        compiler_params=_cparams(2),
        name="gated_merge",
    )(*args)


def _excl_cumsum_lanes(x, tri):
    R, n = x.shape
    nc = n // 128
    xs = jnp.concatenate([x[:, c * 128:(c + 1) * 128] for c in range(nc)], axis=0)
    incl = _dot(xs.astype(BF), tri)
    excl = incl - xs
    tot = incl[:, 127:128]
    off = jnp.zeros((R, 1), F32)
    pieces = []
    for c in range(nc):
        pieces.append(excl[c * R:(c + 1) * R] + off)
        off = off + tot[c * R:(c + 1) * R]
    return jnp.concatenate(pieces, axis=1)


MIN_NORMAL = 2.0 ** -126
MANTISSA_STEPS = 30


def _select_top(a, cap, tri):
    def count_ge(c):
        return jnp.sum(jnp.where(a >= c, 1.0, 0.0), axis=-1, keepdims=True)

    has = count_ge(MIN_NORMAL) >= cap
    p = jnp.full((a.shape[0], 1), MIN_NORMAL, F32)
    for s in (64, 32, 16, 8, 4, 2, 1):
        cand = p * (2.0 ** s)
        p = jnp.where(count_ge(cand) >= cap, cand, p)
    lo = jnp.where(has, p, 0.0)
    hi = jnp.where(has, 2.0 * p, MIN_NORMAL)
    for _ in range(MANTISSA_STEPS):
        mid = 0.5 * (lo + hi)
        ok = count_ge(mid) >= cap
        lo = jnp.where(ok, mid, lo)
        hi = jnp.where(ok, hi, mid)
    above = a >= hi
    edge = jnp.where(a >= lo, jnp.where(above, 0.0, 1.0), 0.0)
    need = cap - jnp.sum(jnp.where(above, 1.0, 0.0), axis=-1, keepdims=True)
    edge_rank = _excl_cumsum_lanes(edge, tri)
    sel = jnp.where(above, 1.0, jnp.where(edge_rank < need, edge, 0.0))
    return sel > 0.5, _excl_cumsum_lanes(sel, tri)


def _select_kernel(aff_ref, tri_ref, slot_ref, *, cap):
    sel, rank = _select_top(aff_ref[...], cap, tri_ref[...])
    slot_ref[...] = jnp.where(sel, rank, -1.0).astype(jnp.int32)


def _select(aff_rows, tri, *, n, col_blk, cap):
    R = aff_rows.shape[0]
    return pl.pallas_call(
        functools.partial(_select_kernel, cap=cap),
        grid=(1,),
        in_specs=[pl.BlockSpec((R, n), lambda i: (0, col_blk)),
                  pl.BlockSpec((128, 128), lambda i: (0, 0))],
        out_specs=pl.BlockSpec((R, n), lambda i: (0, 0)),
        out_shape=jax.ShapeDtypeStruct((R, n), jnp.int32),
        compiler_params=_cparams(1),
        name="ec_select",
    )(aff_rows, tri)


def _gather_rows(slot, aff, u, cap):
    n = slot.shape[1]
    hit = slot == lax.broadcasted_iota(jnp.int32, (cap, n), 0)
    onehot = jnp.where(hit, 1.0, 0.0).astype(BF)
    xs = _dot(onehot, u).astype(BF)
    w = jnp.sum(jnp.where(hit, aff, 0.0), axis=-1, keepdims=True)
    return xs, w


def _swiglu(xs, wg, wu, wd):
    a = _dot(xs, wg)
    up = _dot(xs, wu)
    hm = (a * jax.nn.sigmoid(a)) * up
    return _dot(hm.astype(BF), wd)


def _ffn_lat_kernel(u_ref, slot_ref, aff_ref, wg_ref, wu_ref, wd_ref, y_ref, wg_s, wu_s, wd_s):
    @pl.when(pl.program_id(1) == 0)
    def _():
        wg_s[...] = wg_ref[...].astype(BF)
        wu_s[...] = wu_ref[...].astype(BF)
        wd_s[...] = wd_ref[...].astype(BF)

    xs, w = _gather_rows(slot_ref[...], aff_ref[...], u_ref[...], CAP_LAT)
    y = _swiglu(xs, wg_s[...], wu_s[...], wd_s[...])
    y_ref[...] = (y * w).astype(BF)


def _ffn_ctx_kernel(u_ref, slot_ref, aff_ref, wg_ref, wu_ref, wd_ref, y_ref):
    B = u_ref.shape[0]
    xs_l, w_l = [], []
    for b in range(B):
        xs, w = _gather_rows(slot_ref[b], aff_ref[b], u_ref[b], CAP_CTX)
        xs_l.append(xs)
        w_l.append(w)
    y = _swiglu(jnp.concatenate(xs_l, axis=0), wg_ref[...].astype(BF), wu_ref[...].astype(BF),
                wd_ref[...].astype(BF))
    y = y * jnp.concatenate(w_l, axis=0)
    for b in range(B):
        y_ref[b] = y[b * CAP_CTX:(b + 1) * CAP_CTX].astype(BF)


def _ffn_lat(u, slot_em, aff_em, l, wg, wu, wd):
    B = u.shape[0]
    F = wg.shape[-1]
    row = lambda: pl.BlockSpec((None, None, 1, SEQ), lambda e, b: (b, e, 0, 0))
    wspec = lambda s: pl.BlockSpec((None, None) + s, lambda e, b: (l, e, 0, 0))
    return pl.pallas_call(
        _ffn_lat_kernel,
        grid=(N_EXPERTS, B),
        in_specs=[pl.BlockSpec((None, SEQ, D_MODEL), lambda e, b: (b, 0, 0)), row(), row(),
                  wspec((D_MODEL, F)), wspec((D_MODEL, F)), wspec((F, D_MODEL))],
        out_specs=pl.BlockSpec((None, CAP_LAT, D_MODEL), lambda e, b: (b, e, 0)),
        out_shape=jax.ShapeDtypeStruct((B, N_EXPERTS * CAP_LAT, D_MODEL), BF),
        scratch_shapes=[pltpu.VMEM((D_MODEL, F), BF), pltpu.VMEM((D_MODEL, F), BF),
                        pltpu.VMEM((F, D_MODEL), BF)],
        compiler_params=_cparams(2),
        name="ec_ffn_latent",
    )(u, slot_em, aff_em, wg, wu, wd)


def _ffn_ctx(u, slot_em, aff_em, l, wg, wu, wd):
    B = u.shape[0]
    F = wg.shape[-1]
    ctx_blk = SEQ // CTX_LEN
    row = lambda c: pl.BlockSpec((B, None, 1, CTX_LEN), lambda e: (0, e, 0, c))
    wspec = lambda s: pl.BlockSpec((None, None) + s, lambda e: (l, e, 0, 0))
    return pl.pallas_call(
        _ffn_ctx_kernel,
        grid=(N_EXPERTS,),
        in_specs=[pl.BlockSpec((B, CTX_LEN, D_MODEL), lambda e: (0, ctx_blk, 0),
                               pipeline_mode=pl.Buffered(1)), row(0), row(ctx_blk),
                  wspec((D_MODEL, F)), wspec((D_MODEL, F)), wspec((F, D_MODEL))],
        out_specs=pl.BlockSpec((B, CAP_CTX, D_MODEL), lambda e: (0, e, 0)),
        out_shape=jax.ShapeDtypeStruct((B, N_EXPERTS * CAP_CTX, D_MODEL), BF),
        compiler_params=_cparams(1),
        name="ec_ffn_context",
    )(u, slot_em, aff_em, wg, wu, wd)


def _scatter_matrix(slot_tm, cap):
    rows = slot_tm.shape[0]
    if cap % 128 == 0:
        lane = lax.broadcasted_iota(jnp.int32, (rows, cap), 1)
        pieces = [jnp.where(slot_tm[:, e:e + 1] == lane, 1.0, 0.0).astype(BF)
                  for e in range(N_EXPERTS)]
        return jnp.concatenate(pieces, axis=1)
    lane = lax.broadcasted_iota(jnp.int32, (rows, N_EXPERTS * cap), 1)
    acc = jnp.zeros((rows, N_EXPERTS * cap), F32)
    for e in range(N_EXPERTS):
        s = slot_tm[:, e:e + 1]
        target = jnp.where(s >= 0, s + e * cap, -1)
        acc = acc + jnp.where(target == lane, 1.0, 0.0)
    return acc.astype(BF)


def _scatter_kernel(h_ref, mod_ref, slot_ref, y_ref, g_ref, b_ref, o_ref, *, cap):
    slot = slot_ref[...].astype(F32)
    pad = jnp.full((R_LANES - N_EXPERTS, slot.shape[1]), -1.0, F32)
    slot_tm = jnp.concatenate([slot, pad], axis=0).T.astype(jnp.int32)
    f = _dot(_scatter_matrix(slot_tm, cap), y_ref[...])
    m = mod_ref[...]
    o_ref[...] = _layer_norm(DN_ALPHA * h_ref[...] + m[5:6] * f, g_ref[...], b_ref[...])


def _scatter(h1, mod, l, slot, y, ln_g, ln_b, *, n, tile, row_blk0, mod_row, cap):
    B = h1.shape[0]
    mrow = (lambda b: b) if mod_row is None else (lambda b: mod_row)
    return pl.pallas_call(
        functools.partial(_scatter_kernel, cap=cap),
        grid=(B, n // tile),
        in_specs=[pl.BlockSpec((None, tile, D_MODEL), lambda b, t: (b, row_blk0 + t, 0)),
                  pl.BlockSpec((None, None, N_MOD, D_MODEL), lambda b, t: (l, mrow(b), 0, 0)),
                  pl.BlockSpec((None, N_EXPERTS, tile), lambda b, t: (b, 0, t)),
                  pl.BlockSpec((None, N_EXPERTS * cap, D_MODEL), lambda b, t: (b, 0, 0)),
                  pl.BlockSpec((None, 1, D_MODEL), lambda b, t: (l, 0, 0)),
                  pl.BlockSpec((None, 1, D_MODEL), lambda b, t: (l, 0, 0))],
        out_specs=pl.BlockSpec((None, tile, D_MODEL), lambda b, t: (b, t, 0)),
        out_shape=jax.ShapeDtypeStruct((B, n, D_MODEL), F32),
        compiler_params=_cparams(2),
        name="ec_scatter_norm",
    )(h1, mod, slot, y, ln_g, ln_b)


def _rope_tables():
    t = jnp.arange(SEQ)
    axis_dim = HEAD_DIM // 2
    inv = ROPE_THETA ** (-jnp.arange(0, axis_dim, 2, dtype=F32) / axis_dim)
    ang_r = (t // GRID_W).astype(F32)[:, None] * inv
    ang_c = (t % GRID_W).astype(F32)[:, None] * inv
    cr, sr, cc, sc = jnp.cos(ang_r), jnp.sin(ang_r), jnp.cos(ang_c), jnp.sin(ang_c)
    cos_h = jnp.concatenate([cr, cr, cc, cc], axis=-1)
    sin_h = jnp.concatenate([-sr, sr, -sc, sc], axis=-1)
    cos_t = jnp.concatenate([jnp.tile(cos_h, (1, N_Q_HEADS)), jnp.ones((CTX_LEN, Q_W), F32)], axis=0)
    sin_t = jnp.concatenate([jnp.tile(sin_h, (1, N_Q_HEADS)), jnp.zeros((CTX_LEN, Q_W), F32)], axis=0)
    return cos_t, sin_t


def _dft_tables(n):
    k = np.arange(n, dtype=np.int64)
    ang = 2.0 * np.pi * ((k[:, None] * k[None, :]) % n).astype(np.float64) / n
    cs = np.concatenate([np.cos(ang), np.sin(ang)], axis=1).astype(np.float32)
    c = np.arange(F_W)
    same = (c[:, None] // FOURIER_GROUP_W) == (c[None, :] // FOURIER_GROUP_W)
    angc = 2.0 * np.pi * (((c[:, None] % FOURIER_GROUP_W) * (c[None, :] % FOURIER_GROUP_W))
                          % FOURIER_GROUP_W) / FOURIER_GROUP_W
    cc = np.where(same, np.cos(angc), 0.0).astype(np.float32)
    sc = np.where(same, np.sin(angc), 0.0).astype(np.float32)
    return cs, cc, sc


def _na_bias_table(rpb):
    rows = SEQ // GRID_W
    q_rows, k_rows = QB // GRID_W, NA_SPAN // GRID_W
    n_dc = GRID_W - NA_KW
    padded = jnp.pad(rpb.astype(F32), ((0, 0), (0, 0), (n_dc, n_dc)))
    toe = jnp.stack([padded[:, :, GRID_W - 1 - c:2 * GRID_W - 1 - c] for c in range(GRID_W)], axis=2)
    zero_blk = jnp.zeros((N_Q_HEADS, GRID_W, GRID_W), F32)
    col = np.arange(GRID_W)
    c0 = np.clip(col - NA_KW // 2, 0, GRID_W - NA_KW)
    col_ok = (col[None, :] >= c0[:, None]) & (col[None, :] < c0[:, None] + NA_KW)
    tables = []
    for j in (0, 1, 2, N_LAT_QB - 2, N_LAT_QB - 1):
        kb = min(max(j - 2, 0), N_LAT_QB - 5)
        q_parts = []
        for qr in range(q_rows):
            r = q_rows * j + qr
            r0 = min(max(r - NA_KH // 2, 0), rows - NA_KH)
            k_parts = []
            for ki in range(k_rows):
                kr = q_rows * kb + ki
                if r0 <= kr < r0 + NA_KH:
                    k_parts.append(jnp.where(col_ok[None], toe[:, kr - r + NA_KH - 1], NEG_INF))
                else:
                    k_parts.append(zero_blk + NEG_INF)
            q_parts.append(jnp.concatenate(k_parts, axis=-1))
        tables.append(jnp.concatenate(q_parts, axis=-2))
    tab = jnp.stack(tables, axis=0)
    return jnp.concatenate([tab, jnp.zeros(tab.shape[:3] + (CTX_LEN,), F32)], axis=-1)


def kernel(x, c, ctx, c_ctx, w_mod, b_mod, w_in, qk_gain, sink_logit, na_rpb, w_branch, w_out,
           ln1_g, ln1_b, w_router, w_gate, w_up, w_down, ln2_g, ln2_b):
    B = x.shape[0]
    L = w_mod.shape[0]
    assert x.shape[1:] == (SEQ, D_MODEL) and ctx.shape[1:] == (CTX_LEN, D_MODEL)

    n_rows = -(-(B + 1) // 8) * 8
    c_all = jnp.concatenate([c, c_ctx[None, :], jnp.zeros((n_rows - B - 1, D_MODEL), F32)], axis=0)
    mod_all = _modulation(c_all, w_mod, b_mod).reshape(L, n_rows, N_MOD, D_MODEL)

    cos_t, sin_t = _rope_tables()
    gidx = np.arange(128)
    bd = jnp.asarray(np.where((gidx[:, None] // HEAD_DIM) == (gidx[None, :] // HEAD_DIM),
                              1.0 / HEAD_DIM, 0.0), BF)
    tri = jnp.asarray(gidx[:, None] <= gidx[None, :], BF)
    cs_lat, cc, sc = _dft_tables(SEQ)
    cs_lat, cc, sc = jnp.asarray(cs_lat).astype(BF), jnp.asarray(cc).astype(BF), jnp.asarray(sc).astype(BF)
    cs_ctx = jnp.asarray(_dft_tables(CTX_LEN)[0]).astype(BF)

    w_mix_bf = w_in[:, :, :MIX_W].astype(BF)
    w_gatecols_bf = w_in[:, :, MIX_W:].astype(BF)
    w_branch_bf = w_branch.astype(BF)
    w_out_bf = w_out.astype(BF)
    wr_hi = w_router.astype(BF)
    wr_lo = (w_router - wr_hi.astype(F32)).astype(BF)
    wr_split = jnp.concatenate(
        [wr_hi, wr_lo, jnp.zeros((L, D_MODEL, R_LANES - 2 * N_EXPERTS), BF)], axis=-1)
    ln1_g, ln1_b, ln2_g, ln2_b = (a[:, None, :] for a in (ln1_g, ln1_b, ln2_g, ln2_b))
    per_expert = lambda a: a.reshape(a.shape[0], N_EXPERTS, 1, a.shape[-1])

    h_lat, h_ctx = x, ctx
    for l in range(L):
        with_ctx = l < L - 1
        gq = jnp.tile(qk_gain[l, 0], 2)[None, :]
        gk = jnp.tile(qk_gain[l, 1], 2)[None, :]
        q_all, kv_all = _project(h_lat, h_ctx, mod_all, l, w_mix_bf, gq, gk, cos_t, sin_t, bd)

        br_acd = _mixers(MIXER_KINDS, q_all, kv_all, sink_logit[l], _na_bias_table(na_rpb[l]), with_ctx)
        brb_lat = _fourier(kv_all, cs_lat, cc, sc, row0=0, n=SEQ, tq=512)
        brb_ctx = _fourier(kv_all, cs_ctx, cc, sc, row0=SEQ, n=CTX_LEN, tq=CTX_LEN) if with_ctx else None

        h1, u2, aff_em = _merge(h_lat, h_ctx, mod_all, l, br_acd, brb_lat, brb_ctx,
                                w_gatecols_bf, w_branch_bf, w_out_bf, ln1_g, ln1_b, wr_split)
        aff_rows = aff_em.reshape(B * N_EXPERTS, aff_em.shape[-1])
        aff_pe = per_expert(aff_em)

        slot = _select(aff_rows, tri, n=SEQ, col_blk=0, cap=CAP_LAT).reshape(B, N_EXPERTS, SEQ)
        y_lat = _ffn_lat(u2, per_expert(slot), aff_pe, l, w_gate, w_up, w_down)
        h_lat = _scatter(h1, mod_all, l, slot, y_lat, ln2_g, ln2_b,
                         n=SEQ, tile=TM_SCATTER, row_blk0=0, mod_row=None, cap=CAP_LAT)
        if with_ctx:
            slot_c = _select(aff_rows, tri, n=CTX_LEN, col_blk=SEQ // CTX_LEN,
                             cap=CAP_CTX).reshape(B, N_EXPERTS, CTX_LEN)
            y_ctx = _ffn_ctx(u2, per_expert(slot_c), aff_pe, l, w_gate, w_up, w_down)
            h_ctx = _scatter(h1, mod_all, l, slot_c, y_ctx, ln2_g, ln2_b, n=CTX_LEN, tile=CTX_LEN,
                             row_blk0=SEQ // CTX_LEN, mod_row=B, cap=CAP_CTX)
    return h_lat
```

```python
import functools

import numpy as np
import jax
import jax.numpy as jnp
from jax import lax
from jax.experimental import pallas as pl
from jax.experimental.pallas import tpu as pltpu

D_MODEL = 1024
SEQ = 2048
DEPTH = 2
CTX_LEN = 256
GRID_W = 64
HEAD_DIM = 64
N_Q_HEADS = 4
N_KV_HEADS = 2
Q_W = 256
KV_W = 128
F_W = 256
FOURIER_GROUP_W = 64
N_BRANCH = 4
MIX_W = 3 * (Q_W + 2 * KV_W) + F_W
GATE_W = N_BRANCH * D_MODEL
WINDOW = 128
NA_KH = 8
NA_KW = 16
ROPE_THETA = 10000.0
N_EXPERTS = 16
EC_CAPACITY = 2
N_MOD = 6
DN_ALPHA = (2 * DEPTH) ** 0.25
LN_EPS = 1e-6
RMS_EPS = 1e-6
NEG_INF = -1e30
LOG2E = 1.4426950408889634

N_TOK = SEQ + CTX_LEN
TM = 256
QB = 128
NQ = 2
MIXER_KINDS = ("global", "window", "na")
TM_SCATTER = 512
N_LAT_T = SEQ // TM
N_LAT_QB = SEQ // QB
WIN_SPAN = QB + 2 * WINDOW
NA_SPAN = 640
NA_TYPES = 5
CAP_LAT = EC_CAPACITY * SEQ // N_EXPERTS
CAP_CTX = EC_CAPACITY * CTX_LEN // N_EXPERTS
R_LANES = 128

BF = jnp.bfloat16
F32 = jnp.float32
VMEM_LIMIT = 56 * 1024 * 1024


def _cparams(n_axes):
    return pltpu.CompilerParams(dimension_semantics=("arbitrary",) * n_axes,
                                vmem_limit_bytes=VMEM_LIMIT)


def _dot(a, b):
    return jnp.dot(a, b, preferred_element_type=F32)


def _dot_nt(a, b):
    return lax.dot_general(a, b, (((1,), (1,)), ((), ())), preferred_element_type=F32)


def _layer_norm(x, g, b):
    mu = jnp.mean(x, axis=-1, keepdims=True)
    xc = x - mu
    var = jnp.mean(xc * xc, axis=-1, keepdims=True)
    return xc * lax.rsqrt(var + LN_EPS) * g + b


def _mod_kernel(c_ref, w_ref, b_ref, o_ref):
    c = c_ref[...]
    a = c * jax.nn.sigmoid(c)
    o_ref[...] = jnp.dot(a, w_ref[...], precision=lax.Precision.HIGHEST,
                         preferred_element_type=F32) + b_ref[...]


def _modulation(c_all, w_mod, b_mod):
    L = w_mod.shape[0]
    R = c_all.shape[0]
    ncol = N_MOD * D_MODEL
    tn = D_MODEL
    return pl.pallas_call(
        _mod_kernel,
        grid=(L, ncol // tn),
        in_specs=[pl.BlockSpec((R, D_MODEL), lambda l, j: (0, 0)),
                  pl.BlockSpec((None, D_MODEL, tn), lambda l, j: (l, 0, j)),
                  pl.BlockSpec((None, 1, tn), lambda l, j: (l, 0, j))],
        out_specs=pl.BlockSpec((None, R, tn), lambda l, j: (l, 0, j)),
        out_shape=jax.ShapeDtypeStruct((L, R, ncol), F32),
        compiler_params=_cparams(2),
        name="modulation",
    )(c_all, w_mod, b_mod.reshape(L, 1, ncol))


def _rope128(x, cos, sin):
    lane = lax.broadcasted_iota(jnp.int32, x.shape, 1)
    first = (lane % 32) < 16
    up = pltpu.roll(x, 128 - 16, axis=1)
    dn = pltpu.roll(x, 16, axis=1)
    return x * cos + jnp.where(first, up, dn) * sin


def _rms128(x, gain, bd):
    x2 = x * x
    hi = x2.astype(BF)
    lo = (x2 - hi.astype(F32)).astype(BF)
    ms = _dot(hi, bd) + _dot(lo, bd)
    return x * lax.rsqrt(ms + RMS_EPS) * gain


def _q_slabs(x0, x1):
    lane = lax.broadcasted_iota(jnp.int32, x0.shape, 1)
    lo = lane < HEAD_DIM
    x0r = pltpu.roll(x0, HEAD_DIM, axis=1)
    x1r = pltpu.roll(x1, HEAD_DIM, axis=1)
    scale = HEAD_DIM ** -0.5 * LOG2E
    slabs = [jnp.where(lo, x0, 0.0), jnp.where(lo, x0r, 0.0),
             jnp.where(lo, 0.0, x1r), jnp.where(lo, 0.0, x1)]
    return jnp.concatenate([(s * scale).astype(BF) for s in slabs], axis=1)


def _proj_kernel(hl_ref, hc_ref, mod_ref, w_ref, gq_ref, gk_ref, cos_ref, sin_ref, bd_ref,
                 q_ref, kv_ref):
    m = mod_ref[...]
    h = jnp.where(pl.program_id(1) == N_LAT_T, hc_ref[...], hl_ref[...])
    u = h * (1.0 + m[1:2]) + m[0:1]
    p = _dot(u.astype(BF), w_ref[...])
    cos = cos_ref[...]
    sin = sin_ref[...]
    bd = bd_ref[...]
    gq = gq_ref[...]
    gk = gk_ref[...]
    c0, s0 = cos[:, :128], sin[:, :128]
    c1, s1 = cos[:, 128:], sin[:, 128:]

    qa0 = _rope128(_rms128(p[:, 0:128], gq, bd), c0, s0)
    qa1 = _rope128(_rms128(p[:, 128:256], gq, bd), c1, s1)
    q_ref[:, 0:512] = _q_slabs(qa0, qa1)
    q_ref[:, 512:1024] = _q_slabs(_rope128(p[:, 768:896], c0, s0), _rope128(p[:, 896:1024], c1, s1))
    q_ref[:, 1024:1536] = _q_slabs(p[:, 1280:1408], p[:, 1408:1536])
    kv_ref[:, 0:128] = _rope128(_rms128(p[:, 256:384], gk, bd), c0, s0).astype(BF)
    kv_ref[:, 128:256] = p[:, 384:512].astype(BF)
    kv_ref[:, 256:384] = _rope128(p[:, 1024:1152], c0, s0).astype(BF)
    kv_ref[:, 384:512] = p[:, 1152:1280].astype(BF)
    kv_ref[:, 512:768] = p[:, 1536:1792].astype(BF)
    kv_ref[:, 768:1024] = p[:, 512:768].astype(BF)


QS_W = 2 * Q_W
KV_COL = {"global": (0, 1), "window": (2, 3), "na": (4, 5)}
Q_COL = {"global": 0, "window": 1, "na": 2}
ACT_W = 1024
FB_COL = 3


def _lat_ctx_specs(h_lat, h_ctx):
    return ([pl.BlockSpec((None, TM, D_MODEL), lambda b, t: (b, jnp.minimum(t, N_LAT_T - 1), 0)),
             pl.BlockSpec((None, TM, D_MODEL), lambda b, t: (b, 0, 0))], [h_lat, h_ctx])


def _project(h_lat, h_ctx, mod, l, w_mix, gq, gk, cos_t, sin_t, bd):
    B = h_lat.shape[0]
    n_t = N_TOK // TM
    ctx_row = B
    tok = lambda w: pl.BlockSpec((None, TM, w), lambda b, t: (b, t, 0))
    const = lambda shape: pl.BlockSpec(shape, lambda b, t: (0,) * len(shape))
    widths = (3 * QS_W, ACT_W)
    h_specs, h_args = _lat_ctx_specs(h_lat, h_ctx)
    return pl.pallas_call(
        _proj_kernel,
        grid=(B, n_t),
        in_specs=h_specs + [
            pl.BlockSpec((None, None, N_MOD, D_MODEL),
                         lambda b, t: (l, jnp.where(t == N_LAT_T, ctx_row, b), 0, 0)),
            pl.BlockSpec((None, D_MODEL, MIX_W), lambda b, t: (l, 0, 0)),
            const((1, 128)), const((1, 128)),
            pl.BlockSpec((TM, Q_W), lambda b, t: (t, 0)),
            pl.BlockSpec((TM, Q_W), lambda b, t: (t, 0)),
            const((128, 128))],
        out_specs=[tok(w) for w in widths],
        out_shape=[jax.ShapeDtypeStruct((B, N_TOK, w), BF) for w in widths],
        compiler_params=_cparams(2),
        name="mixer_projection",
    )(*h_args, mod, w_mix, gq, gk, cos_t, sin_t, bd)


def _fourier_kernel(x_ref, cs_ref, cc_ref, sc_ref, o_ref, z_ref, *, row0, n):
    @pl.when(pl.program_id(1) == 0)
    def _():
        x = x_ref[row0:row0 + n, :]
        z_ref[0:n, :] = _dot(x, cc_ref[...]).astype(BF)
        z_ref[n:2 * n, :] = (-_dot(x, sc_ref[...])).astype(BF)

    scale = (n * FOURIER_GROUP_W) ** -0.5
    o_ref[...] = (_dot(cs_ref[...], z_ref[...]) * scale).astype(BF)


def _fourier(fb_all, cs, cc, sc, *, row0, n, tq):
    B = fb_all.shape[0]
    return pl.pallas_call(
        functools.partial(_fourier_kernel, row0=row0, n=n),
        grid=(B, n // tq),
        in_specs=[pl.BlockSpec((None, N_TOK, F_W), lambda b, t: (b, 0, FB_COL)),
                  pl.BlockSpec((tq, 2 * n), lambda b, t: (t, 0)),
                  pl.BlockSpec((F_W, F_W), lambda b, t: (0, 0)),
                  pl.BlockSpec((F_W, F_W), lambda b, t: (0, 0))],
        out_specs=pl.BlockSpec((None, tq, F_W), lambda b, t: (b, t, 0)),
        out_shape=jax.ShapeDtypeStruct((B, n, F_W), BF),
        scratch_shapes=[pltpu.VMEM((2 * n, F_W), BF)],
        compiler_params=_cparams(2),
        name="fourier_mixer",
    )(fb_all, cs, cc, sc)


def _attend(q, kcat, vcat, add_bias, sinks):
    outs = []
    for h in range(N_KV_HEADS):
        qs = jnp.concatenate([q[:, (2 * h) * 128:(2 * h + 1) * 128],
                              q[:, (2 * h + 1) * 128:(2 * h + 2) * 128]], axis=0)
        s = add_bias(_dot_nt(qs, kcat), h)
        m = jnp.max(s, axis=-1, keepdims=True)
        if sinks is not None:
            row = lax.broadcasted_iota(jnp.int32, (2 * QB, 1), 0)
            sk = jnp.where(row < QB, sinks[2 * h], sinks[2 * h + 1])
            m = jnp.maximum(m, sk)
        p = jnp.exp2(s - m)
        l = jnp.sum(p, axis=-1, keepdims=True)
        if sinks is not None:
            l = l + jnp.exp2(sk - m)
        outs.append(_dot(p.astype(BF), vcat) / l)
    o0, o1 = outs
    lane = lax.broadcasted_iota(jnp.int32, (QB, 128), 1)
    lo = lane < HEAD_DIM
    left = jnp.where(lo, o0[:QB], pltpu.roll(o0[QB:], HEAD_DIM, axis=1))
    right = jnp.where(lo, pltpu.roll(o1[:QB], HEAD_DIM, axis=1), o1[QB:])
    return jnp.concatenate([left, right], axis=1)


def _mixers_kernel(*refs, kinds, with_ctx):
    refs = list(refs)
    sinks = None
    if "window" in kinds:
        sink_ref = refs.pop(0)
        sinks = [sink_ref[i] * LOG2E for i in range(N_Q_HEADS)]
    io = {}
    for kind in kinds:
        io[kind] = [refs.pop(0), refs.pop(0), refs.pop(0)]
    if "na" in kinds:
        tab_ref = refs.pop(0)
    out_ref = refs.pop(0)
    for n, kind in enumerate(kinds):
        io[kind].append(out_ref.at[:, n * Q_W:(n + 1) * Q_W])
    step = pl.program_id(1)
    no_bias = lambda s, h: s

    def lat_block(kind, j, q, k_ref, v_ref):
        ctx_k, ctx_v = k_ref[SEQ:N_TOK, :], v_ref[SEQ:N_TOK, :]
        if kind == "global":
            return _attend(q, k_ref[...], v_ref[...], no_bias, None)
        if kind == "window":
            start = pl.multiple_of(QB * jnp.clip(j - 1, 0, N_LAT_QB - 3), QB)
            kcat = jnp.concatenate([k_ref[pl.ds(start, WIN_SPAN), :], ctx_k], axis=0)
            vcat = jnp.concatenate([v_ref[pl.ds(start, WIN_SPAN), :], ctx_v], axis=0)
            nk = WIN_SPAN + CTX_LEN
            col = lax.broadcasted_iota(jnp.int32, (2 * QB, nk), 1)
            row = lax.broadcasted_iota(jnp.int32, (2 * QB, nk), 0)
            q_pos = QB * j + row % QB
            ok = (col >= WIN_SPAN) | (jnp.abs(q_pos - (start + col)) <= WINDOW)
            return _attend(q, kcat, vcat, lambda s, h: jnp.where(ok, s, NEG_INF), sinks)
        start = pl.multiple_of(QB * jnp.clip(j - 2, 0, N_LAT_QB - 5), QB)
        kcat = jnp.concatenate([k_ref[pl.ds(start, NA_SPAN), :], ctx_k], axis=0)
        vcat = jnp.concatenate([v_ref[pl.ds(start, NA_SPAN), :], ctx_v], axis=0)
        tix = jnp.where(j < 2, j, jnp.where(j > N_LAT_QB - 3, j - (N_LAT_QB - 5), 2))

        def na_bias(s, h):
            b = tab_ref[tix, 2 * h:2 * h + 2]
            return s + b.reshape(2 * QB, NA_SPAN + CTX_LEN)

        return _attend(q, kcat, vcat, na_bias, None)

    def latent():
        for i in range(NQ):
            rows = slice(i * QB, (i + 1) * QB)
            for kind in kinds:
                q_ref, k_ref, v_ref, o_ref = io[kind]
                o_ref[rows, :] = lat_block(kind, step * NQ + i, q_ref[rows, :], k_ref, v_ref).astype(BF)

    def context():
        for i in range(NQ):
            rows = slice(i * QB, (i + 1) * QB)
            for kind in kinds:
                q_ref, k_ref, v_ref, o_ref = io[kind]
                o = _attend(q_ref[rows, :], k_ref[SEQ:N_TOK, :], v_ref[SEQ:N_TOK, :], no_bias,
                            sinks if kind == "window" else None)
                o_ref[rows, :] = o.astype(BF)

    if with_ctx:
        pl.when(step < N_LAT_QB // NQ)(latent)
        pl.when(step >= N_LAT_QB // NQ)(context)
    else:
        latent()


def _mixers(kinds, q_all, kv_all, sink, na_table, with_ctx):
    B = q_all.shape[0]
    n_steps = (N_LAT_QB + (CTX_LEN // QB if with_ctx else 0)) // NQ
    in_specs, args = [], []
    if "window" in kinds:
        in_specs.append(pl.BlockSpec(memory_space=pltpu.SMEM))
        args.append(sink)
    for kind in kinds:
        qc, (kc, vc) = Q_COL[kind], KV_COL[kind]
        in_specs += [pl.BlockSpec((None, NQ * QB, QS_W), lambda b, j, c=qc: (b, j, c)),
                     pl.BlockSpec((None, N_TOK, KV_W), lambda b, j, c=kc: (b, 0, c)),
                     pl.BlockSpec((None, N_TOK, KV_W), lambda b, j, c=vc: (b, 0, c))]
        args += [q_all, kv_all, kv_all]
    if "na" in kinds:
        in_specs.append(pl.BlockSpec(na_table.shape, lambda b, j: (0, 0, 0, 0)))
        args.append(na_table)
    width = Q_W * len(kinds)
    return pl.pallas_call(
        functools.partial(_mixers_kernel, kinds=kinds, with_ctx=with_ctx),
        grid=(B, n_steps),
        in_specs=in_specs,
        out_specs=pl.BlockSpec((None, NQ * QB, width), lambda b, j: (b, j, 0)),
        out_shape=jax.ShapeDtypeStruct((B, n_steps * NQ * QB, width), BF),
        compiler_params=_cparams(2),
        name="attention_" + "_".join(kinds),
    )(*args)


def _merge_kernel(*refs, with_ctx):
    if with_ctx:
        (h_ref, hc_ref, mod_ref, acd_ref, bl_ref, bc_ref,
         wg_ref, wb_ref, wo_ref, g_ref, b_ref, wr_ref, h1_ref, aff_ref, u_ref) = refs
    else:
        (h_ref, mod_ref, acd_ref, bl_ref,
         wg_ref, wb_ref, wo_ref, g_ref, b_ref, wr_ref, h1_ref, aff_ref) = refs
        u_ref = None
    m = mod_ref[...]
    h = h_ref[...]
    br_b = bl_ref[...]
    if with_ctx:
        is_ctx = pl.program_id(1) == N_LAT_T
        h = jnp.where(is_ctx, hc_ref[...], h)
        br_b = jnp.where(is_ctx, bc_ref[...], br_b)
    u = (h * (1.0 + m[1:2]) + m[0:1]).astype(BF)
    acd = acd_ref[...]
    branches = (acd[:, 0:Q_W], br_b, acd[:, Q_W:2 * Q_W], acd[:, 2 * Q_W:3 * Q_W])
    merged = None
    for i in range(N_BRANCH):
        gate = jax.nn.sigmoid(_dot(u, wg_ref[:, i * D_MODEL:(i + 1) * D_MODEL]))
        term = gate * _dot(branches[i], wb_ref[i])
        merged = term if merged is None else merged + term
    y = _dot(merged.astype(BF), wo_ref[...])
    h1 = _layer_norm(DN_ALPHA * h + m[2:3] * y, g_ref[...], b_ref[...])
    h1_ref[...] = h1

    u2 = h1 * (1.0 + m[4:5]) + m[3:4]
    u_hi = u2.astype(BF)
    if u_ref is not None:
        u_ref[...] = u_hi
    u_lo = (u2 - u_hi.astype(F32)).astype(BF)
    part = _dot(u_hi, wr_ref[...]) + _dot(u_lo, wr_ref[...])
    logits = part + pltpu.roll(part, R_LANES - N_EXPERTS, axis=1)
    lane = lax.broadcasted_iota(jnp.int32, logits.shape, 1)
    lg = jnp.where(lane < N_EXPERTS, logits, NEG_INF)
    ex = jnp.exp(lg - jnp.max(lg, axis=-1, keepdims=True))
    aff = ex / jnp.sum(ex, axis=-1, keepdims=True)
    aff_ref[...] = aff.T[0:N_EXPERTS]


def _merge(h_lat, h_ctx, mod, l, br_acd, brb_lat, brb_ctx, w_gate, w_branch, w_out, ln_g, ln_b, w_router_split):
    B = h_lat.shape[0]
    with_ctx = brb_ctx is not None
    n_t = N_LAT_T + (1 if with_ctx else 0)
    tok = lambda w: pl.BlockSpec((None, TM, w), lambda b, t: (b, t, 0))
    layer = lambda shape: pl.BlockSpec((None,) + shape, lambda b, t: (l,) + (0,) * len(shape))
    if with_ctx:
        in_specs, args = _lat_ctx_specs(h_lat, h_ctx)
    else:
        in_specs, args = [tok(D_MODEL)], [h_lat]
    in_specs += [pl.BlockSpec((None, None, N_MOD, D_MODEL),
                              lambda b, t: (l, jnp.where(t == N_LAT_T, B, b), 0, 0)),
                 tok(3 * Q_W),
                 pl.BlockSpec((None, TM, F_W), lambda b, t: (b, jnp.minimum(t, N_LAT_T - 1), 0))]
    args += [mod, br_acd, brb_lat]
    if with_ctx:
        in_specs.append(pl.BlockSpec((None, TM, F_W), lambda b, t: (b, 0, 0)))
        args.append(brb_ctx)
    in_specs += [layer((D_MODEL, GATE_W)), layer((N_BRANCH, Q_W, D_MODEL)),
                 layer((D_MODEL, D_MODEL)), layer((1, D_MODEL)), layer((1, D_MODEL)),
                 layer((D_MODEL, R_LANES))]
    args += [w_gate, w_branch, w_out, ln_g, ln_b, w_router_split]
    rows = n_t * TM
    out_specs = [tok(D_MODEL), pl.BlockSpec((None, N_EXPERTS, TM), lambda b, t: (b, 0, t))]
    out_shape = [jax.ShapeDtypeStruct((B, rows, D_MODEL), F32),
                 jax.ShapeDtypeStruct((B, N_EXPERTS, rows), F32)]
    if with_ctx:
        out_specs.append(tok(D_MODEL))
        out_shape.append(jax.ShapeDtypeStruct((B, rows, D_MODEL), BF))
    out = pl.pallas_call(
        functools.partial(_merge_kernel, with_ctx=with_ctx),
        grid=(B, n_t),
        in_specs=in_specs,
        out_specs=out_specs,
        out_shape=out_shape,
        compiler_params=_cparams(2),
        name="gated_merge",
    )(*args)
    return out if with_ctx else (out[0], out[1], None)


def _excl_cumsum_lanes(x, tri):
    R, n = x.shape
    nc = n // 128
    xs = jnp.concatenate([x[:, c * 128:(c + 1) * 128] for c in range(nc)], axis=0)
    incl = _dot(xs.astype(BF), tri)
    excl = incl - xs
    tot = incl[:, 127:128]
    off = jnp.zeros((R, 1), F32)
    pieces = []
    for c in range(nc):
        pieces.append(excl[c * R:(c + 1) * R] + off)
        off = off + tot[c * R:(c + 1) * R]
    return jnp.concatenate(pieces, axis=1)


MIN_NORMAL = 2.0 ** -126
MANTISSA_STEPS = 30


def _select_top(a, cap, tri):
    def count_ge(c):
        return jnp.sum(jnp.where(a >= c, 1.0, 0.0), axis=-1, keepdims=True)

    has = count_ge(MIN_NORMAL) >= cap
    p = jnp.full((a.shape[0], 1), MIN_NORMAL, F32)
    for s in (64, 32, 16, 8, 4, 2, 1):
        cand = p * (2.0 ** s)
        p = jnp.where(count_ge(cand) >= cap, cand, p)
    lo = jnp.where(has, p, 0.0)
    hi = jnp.where(has, 2.0 * p, MIN_NORMAL)
    for _ in range(MANTISSA_STEPS):
        mid = 0.5 * (lo + hi)
        ok = count_ge(mid) >= cap
        lo = jnp.where(ok, mid, lo)
        hi = jnp.where(ok, hi, mid)
    above = a >= hi
    edge = jnp.where(a >= lo, jnp.where(above, 0.0, 1.0), 0.0)
    need = cap - jnp.sum(jnp.where(above, 1.0, 0.0), axis=-1, keepdims=True)
    edge_rank = _excl_cumsum_lanes(edge, tri)
    sel = jnp.where(above, 1.0, jnp.where(edge_rank < need, edge, 0.0))
    return sel > 0.5, _excl_cumsum_lanes(sel, tri)


def _select_kernel(aff_ref, tri_ref, slot_ref, *, cap):
    sel, rank = _select_top(aff_ref[...], cap, tri_ref[...])
    slot_ref[...] = jnp.where(sel, rank, -1.0).astype(jnp.int32)


def _select(aff_rows, tri, *, n, col_blk, cap):
    R = aff_rows.shape[0]
    return pl.pallas_call(
        functools.partial(_select_kernel, cap=cap),
        grid=(1,),
        in_specs=[pl.BlockSpec((R, n), lambda i: (0, col_blk)),
                  pl.BlockSpec((128, 128), lambda i: (0, 0))],
        out_specs=pl.BlockSpec((R, n), lambda i: (0, 0)),
        out_shape=jax.ShapeDtypeStruct((R, n), jnp.int32),
        compiler_params=_cparams(1),
        name="ec_select",
    )(aff_rows, tri)


def _swiglu(xs, wg, wu, wd):
    a = _dot(xs, wg)
    up = _dot(xs, wu)
    hm = (a * jax.nn.sigmoid(a)) * up
    return _dot(hm.astype(BF), wd)


def _index_kernel(slot_ref, idx_ref):
    n = slot_ref.shape[1]
    cap = idx_ref.shape[1]
    slot_tm = _token_major(slot_ref[...].astype(F32)).astype(jnp.int32)
    tok = lax.broadcasted_iota(jnp.int32, (n, cap), 0).astype(F32)
    lane = lax.broadcasted_iota(jnp.int32, (n, cap), 1)
    rows = [jnp.sum(jnp.where(slot_tm[:, e:e + 1] == lane, tok, 0.0), axis=0, keepdims=True)
            for e in range(N_EXPERTS)]
    idx_ref[...] = jnp.concatenate(rows, axis=0).astype(jnp.int32)


def _slot_tokens(slot, cap):
    B, _, n = slot.shape
    return pl.pallas_call(
        _index_kernel,
        grid=(B,),
        in_specs=[pl.BlockSpec((None, N_EXPERTS, n), lambda b: (b, 0, 0))],
        out_specs=pl.BlockSpec((None, N_EXPERTS, cap), lambda b: (b, 0, 0)),
        out_shape=jax.ShapeDtypeStruct((B, N_EXPERTS, cap), jnp.int32),
        compiler_params=_cparams(1),
        name="ec_slot_tokens",
    )(slot)


FFN_CHUNKS = 4


def _gather_copy(h_hbm, xbuf, sem, b, tok, buf, j):
    return pltpu.make_async_copy(h_hbm.at[b, pl.ds(tok, 1), :], xbuf.at[buf, pl.ds(j, 1), :],
                                 sem.at[buf])


def _ffn_lat_kernel(idx_ref, idx_next_ref, h_hbm, mod_ref, wg_ref, wu_ref, wd_ref, y_ref,
                    wg_s, wu_s, wd_s, xbuf, sem):
    e, b = pl.program_id(0), pl.program_id(1)
    n_b = pl.num_programs(1)
    step = e * n_b + b
    buf = step % 2

    def start_rows(idx, bb, into, lo=0, hi=CAP_LAT):
        for j in range(lo, hi):
            _gather_copy(h_hbm, xbuf, sem, bb, idx[0, j], into, j).start()

    def wait_rows(which):
        for j in range(CAP_LAT):
            _gather_copy(h_hbm, xbuf, sem, b, 0, which, j).wait()

    @pl.when(step == 0)
    def _():
        start_rows(idx_ref, b, buf)

    @pl.when(b == 0)
    def _():
        wg_s[...] = wg_ref[...].astype(BF)
        wu_s[...] = wu_ref[...].astype(BF)
        wd_s[...] = wd_ref[...].astype(BF)

    wait_rows(buf)
    m = mod_ref[...]
    xs = (xbuf[buf] * (1.0 + m[4:5]) + m[3:4]).astype(BF)
    next_b = (b + 1) % n_b
    fc = wg_s.shape[1] // FFN_CHUNKS
    per = CAP_LAT // FFN_CHUNKS
    y = None
    for c in range(FFN_CHUNKS):
        start_rows(idx_next_ref, next_b, 1 - buf, c * per, (c + 1) * per)
        a = _dot(xs, wg_s[:, c * fc:(c + 1) * fc])
        up = _dot(xs, wu_s[:, c * fc:(c + 1) * fc])
        hm = ((a * jax.nn.sigmoid(a)) * up).astype(BF)
        part = _dot(hm, wd_s[c * fc:(c + 1) * fc, :])
        y = part if y is None else y + part
    y_ref[...] = y.astype(BF)

    @pl.when(step + 1 == pl.num_programs(0) * n_b)
    def _():
        wait_rows(1 - buf)


def _onehot_rows(slot, u, cap):
    n = slot.shape[1]
    hit = slot == lax.broadcasted_iota(jnp.int32, (cap, n), 0)
    return _dot(jnp.where(hit, 1.0, 0.0).astype(BF), u).astype(BF)


def _ffn_ctx_kernel(u_ref, slot_ref, wg_ref, wu_ref, wd_ref, y_ref):
    B = u_ref.shape[0]
    xs = jnp.concatenate([_onehot_rows(slot_ref[b], u_ref[b], CAP_CTX) for b in range(B)], axis=0)
    y = _swiglu(xs, wg_ref[...].astype(BF), wu_ref[...].astype(BF), wd_ref[...].astype(BF))
    for b in range(B):
        y_ref[b] = y[b * CAP_CTX:(b + 1) * CAP_CTX].astype(BF)


def _ffn_lat(h1, idx, mod, l, wg, wu, wd):
    B = h1.shape[0]
    F = wg.shape[-1]
    wspec = lambda s: pl.BlockSpec((None, None) + s, lambda e, b: (l, e, 0, 0))

    def next_step(e, b):
        nb = (b + 1) % B
        return nb, jnp.minimum(e + (b + 1) // B, N_EXPERTS - 1)

    idx_spec = lambda f: pl.BlockSpec((None, None, 1, CAP_LAT), f, memory_space=pltpu.SMEM)
    return pl.pallas_call(
        _ffn_lat_kernel,
        grid=(N_EXPERTS, B),
        in_specs=[idx_spec(lambda e, b: (b, e, 0, 0)),
                  idx_spec(lambda e, b: next_step(e, b) + (0, 0)),
                  pl.BlockSpec(memory_space=pl.ANY),
                  pl.BlockSpec((None, None, N_MOD, D_MODEL), lambda e, b: (l, b, 0, 0)),
                  wspec((D_MODEL, F)), wspec((D_MODEL, F)), wspec((F, D_MODEL))],
        out_specs=pl.BlockSpec((None, CAP_LAT, D_MODEL), lambda e, b: (b, e, 0)),
        out_shape=jax.ShapeDtypeStruct((B, N_EXPERTS * CAP_LAT, D_MODEL), BF),
        scratch_shapes=[pltpu.VMEM((D_MODEL, F), BF), pltpu.VMEM((D_MODEL, F), BF),
                        pltpu.VMEM((F, D_MODEL), BF),
                        pltpu.VMEM((2, CAP_LAT, D_MODEL), F32), pltpu.SemaphoreType.DMA((2,))],
        compiler_params=_cparams(2),
        name="ec_ffn_latent",
    )(idx, idx, h1, mod, wg, wu, wd)


def _ffn_ctx(u, slot_em, l, wg, wu, wd):
    B = u.shape[0]
    F = wg.shape[-1]
    wspec = lambda s: pl.BlockSpec((None, None) + s, lambda e: (l, e, 0, 0))
    return pl.pallas_call(
        _ffn_ctx_kernel,
        grid=(N_EXPERTS,),
        in_specs=[pl.BlockSpec((B, CTX_LEN, D_MODEL), lambda e: (0, SEQ // CTX_LEN, 0),
                               pipeline_mode=pl.Buffered(1)),
                  pl.BlockSpec((B, None, 1, CTX_LEN), lambda e: (0, e, 0, 0)),
                  wspec((D_MODEL, F)), wspec((D_MODEL, F)), wspec((F, D_MODEL))],
        out_specs=pl.BlockSpec((B, CAP_CTX, D_MODEL), lambda e: (0, e, 0)),
        out_shape=jax.ShapeDtypeStruct((B, N_EXPERTS * CAP_CTX, D_MODEL), BF),
        compiler_params=_cparams(1),
        name="ec_ffn_context",
    )(u, slot_em, wg, wu, wd)


def _token_major(x_em):
    pad = jnp.full((R_LANES - N_EXPERTS, x_em.shape[1]), -1.0, F32)
    return jnp.concatenate([x_em, pad], axis=0).T


def _scatter_matrix(slot_tm, aff_tm, cap):
    rows = slot_tm.shape[0]
    if cap % 128 == 0:
        lane = lax.broadcasted_iota(jnp.int32, (rows, cap), 1)
        pieces = [jnp.where(slot_tm[:, e:e + 1] == lane, aff_tm[:, e:e + 1], 0.0).astype(BF)
                  for e in range(N_EXPERTS)]
        return jnp.concatenate(pieces, axis=1)
    lane = lax.broadcasted_iota(jnp.int32, (rows, N_EXPERTS * cap), 1)
    acc = jnp.zeros((rows, N_EXPERTS * cap), F32)
    for e in range(N_EXPERTS):
        s = slot_tm[:, e:e + 1]
        target = jnp.where(s >= 0, s + e * cap, -1)
        acc = acc + jnp.where(target == lane, aff_tm[:, e:e + 1], 0.0)
    return acc.astype(BF)


def _scatter_kernel(h_ref, mod_ref, slot_ref, aff_ref, y_ref, g_ref, b_ref, o_ref, *, cap):
    slot_tm = _token_major(slot_ref[...].astype(F32)).astype(jnp.int32)
    f = _dot(_scatter_matrix(slot_tm, _token_major(aff_ref[...]), cap), y_ref[...])
    m = mod_ref[...]
    o_ref[...] = _layer_norm(DN_ALPHA * h_ref[...] + m[5:6] * f, g_ref[...], b_ref[...])


def _scatter(h1, mod, l, slot, aff_em, y, ln_g, ln_b, *, n, tile, row_blk0, mod_row, cap):
    B = h1.shape[0]
    mrow = (lambda b: b) if mod_row is None else (lambda b: mod_row)
    return pl.pallas_call(
        functools.partial(_scatter_kernel, cap=cap),
        grid=(B, n // tile),
        in_specs=[pl.BlockSpec((None, tile, D_MODEL), lambda b, t: (b, row_blk0 + t, 0)),
                  pl.BlockSpec((None, None, N_MOD, D_MODEL), lambda b, t: (l, mrow(b), 0, 0)),
                  pl.BlockSpec((None, N_EXPERTS, tile), lambda b, t: (b, 0, t)),
                  pl.BlockSpec((None, N_EXPERTS, tile), lambda b, t: (b, 0, row_blk0 + t)),
                  pl.BlockSpec((None, N_EXPERTS * cap, D_MODEL), lambda b, t: (b, 0, 0)),
                  pl.BlockSpec((None, 1, D_MODEL), lambda b, t: (l, 0, 0)),
                  pl.BlockSpec((None, 1, D_MODEL), lambda b, t: (l, 0, 0))],
        out_specs=pl.BlockSpec((None, tile, D_MODEL), lambda b, t: (b, t, 0)),
        out_shape=jax.ShapeDtypeStruct((B, n, D_MODEL), F32),
        compiler_params=_cparams(2),
        name="ec_scatter_norm",
    )(h1, mod, slot, aff_em, y, ln_g, ln_b)


def _rope_tables():
    t = jnp.arange(SEQ)
    axis_dim = HEAD_DIM // 2
    inv = ROPE_THETA ** (-jnp.arange(0, axis_dim, 2, dtype=F32) / axis_dim)
    ang_r = (t // GRID_W).astype(F32)[:, None] * inv
    ang_c = (t % GRID_W).astype(F32)[:, None] * inv
    cr, sr, cc, sc = jnp.cos(ang_r), jnp.sin(ang_r), jnp.cos(ang_c), jnp.sin(ang_c)
    cos_h = jnp.concatenate([cr, cr, cc, cc], axis=-1)
    sin_h = jnp.concatenate([-sr, sr, -sc, sc], axis=-1)
    cos_t = jnp.concatenate([jnp.tile(cos_h, (1, N_Q_HEADS)), jnp.ones((CTX_LEN, Q_W), F32)], axis=0)
    sin_t = jnp.concatenate([jnp.tile(sin_h, (1, N_Q_HEADS)), jnp.zeros((CTX_LEN, Q_W), F32)], axis=0)
    return cos_t, sin_t


def _dft_tables(n):
    k = np.arange(n, dtype=np.int64)
    ang = 2.0 * np.pi * ((k[:, None] * k[None, :]) % n).astype(np.float64) / n
    cs = np.concatenate([np.cos(ang), np.sin(ang)], axis=1).astype(np.float32)
    c = np.arange(F_W)
    same = (c[:, None] // FOURIER_GROUP_W) == (c[None, :] // FOURIER_GROUP_W)
    angc = 2.0 * np.pi * (((c[:, None] % FOURIER_GROUP_W) * (c[None, :] % FOURIER_GROUP_W))
                          % FOURIER_GROUP_W) / FOURIER_GROUP_W
    cc = np.where(same, np.cos(angc), 0.0).astype(np.float32)
    sc = np.where(same, np.sin(angc), 0.0).astype(np.float32)
    return cs, cc, sc


def _na_bias_table(rpb):
    rows = SEQ // GRID_W
    q_rows, k_rows = QB // GRID_W, NA_SPAN // GRID_W
    n_dc = GRID_W - NA_KW
    padded = jnp.pad(rpb.astype(F32) * LOG2E, ((0, 0), (0, 0), (n_dc, n_dc)))
    toe = jnp.stack([padded[:, :, GRID_W - 1 - c:2 * GRID_W - 1 - c] for c in range(GRID_W)], axis=2)
    zero_blk = jnp.zeros((N_Q_HEADS, GRID_W, GRID_W), F32)
    col = np.arange(GRID_W)
    c0 = np.clip(col - NA_KW // 2, 0, GRID_W - NA_KW)
    col_ok = (col[None, :] >= c0[:, None]) & (col[None, :] < c0[:, None] + NA_KW)
    tables = []
    for j in (0, 1, 2, N_LAT_QB - 2, N_LAT_QB - 1):
        kb = min(max(j - 2, 0), N_LAT_QB - 5)
        q_parts = []
        for qr in range(q_rows):
            r = q_rows * j + qr
            r0 = min(max(r - NA_KH // 2, 0), rows - NA_KH)
            k_parts = []
            for ki in range(k_rows):
                kr = q_rows * kb + ki
                if r0 <= kr < r0 + NA_KH:
                    k_parts.append(jnp.where(col_ok[None], toe[:, kr - r + NA_KH - 1], NEG_INF))
                else:
                    k_parts.append(zero_blk + NEG_INF)
            q_parts.append(jnp.concatenate(k_parts, axis=-1))
        tables.append(jnp.concatenate(q_parts, axis=-2))
    tab = jnp.stack(tables, axis=0)
    return jnp.concatenate([tab, jnp.zeros(tab.shape[:3] + (CTX_LEN,), F32)], axis=-1)


def kernel(x, c, ctx, c_ctx, w_mod, b_mod, w_in, qk_gain, sink_logit, na_rpb, w_branch, w_out,
           ln1_g, ln1_b, w_router, w_gate, w_up, w_down, ln2_g, ln2_b):
    B = x.shape[0]
    L = w_mod.shape[0]
    assert x.shape[1:] == (SEQ, D_MODEL) and ctx.shape[1:] == (CTX_LEN, D_MODEL)

    n_rows = -(-(B + 1) // 8) * 8
    c_all = jnp.concatenate([c, c_ctx[None, :], jnp.zeros((n_rows - B - 1, D_MODEL), F32)], axis=0)
    mod_all = _modulation(c_all, w_mod, b_mod).reshape(L, n_rows, N_MOD, D_MODEL)

    cos_t, sin_t = _rope_tables()
    gidx = np.arange(128)
    bd = jnp.asarray(np.where((gidx[:, None] // HEAD_DIM) == (gidx[None, :] // HEAD_DIM),
                              1.0 / HEAD_DIM, 0.0), BF)
    tri = jnp.asarray(gidx[:, None] <= gidx[None, :], BF)
    cs_lat, cc, sc = _dft_tables(SEQ)
    cs_lat, cc, sc = jnp.asarray(cs_lat).astype(BF), jnp.asarray(cc).astype(BF), jnp.asarray(sc).astype(BF)
    cs_ctx = jnp.asarray(_dft_tables(CTX_LEN)[0]).astype(BF)

    w_mix_bf = w_in[:, :, :MIX_W].astype(BF)
    w_gatecols_bf = w_in[:, :, MIX_W:].astype(BF)
    w_branch_bf = w_branch.astype(BF)
    w_out_bf = w_out.astype(BF)
    wr_hi = w_router.astype(BF)
    wr_lo = (w_router - wr_hi.astype(F32)).astype(BF)
    wr_split = jnp.concatenate(
        [wr_hi, wr_lo, jnp.zeros((L, D_MODEL, R_LANES - 2 * N_EXPERTS), BF)], axis=-1)
    ln1_g, ln1_b, ln2_g, ln2_b = (a[:, None, :] for a in (ln1_g, ln1_b, ln2_g, ln2_b))
    per_expert = lambda a: a.reshape(a.shape[0], N_EXPERTS, 1, a.shape[-1])

    h_lat, h_ctx = x, ctx
    for l in range(L):
        with_ctx = l < L - 1
        gq = jnp.tile(qk_gain[l, 0], 2)[None, :]
        gk = jnp.tile(qk_gain[l, 1], 2)[None, :]
        q_all, kv_all = _project(h_lat, h_ctx, mod_all, l, w_mix_bf, gq, gk, cos_t, sin_t, bd)

        br_acd = _mixers(MIXER_KINDS, q_all, kv_all, sink_logit[l], _na_bias_table(na_rpb[l]), with_ctx)
        brb_lat = _fourier(kv_all, cs_lat, cc, sc, row0=0, n=SEQ, tq=512)
        brb_ctx = _fourier(kv_all, cs_ctx, cc, sc, row0=SEQ, n=CTX_LEN, tq=CTX_LEN) if with_ctx else None

        h1, aff_em, u2 = _merge(h_lat, h_ctx, mod_all, l, br_acd, brb_lat, brb_ctx,
                                w_gatecols_bf, w_branch_bf, w_out_bf, ln1_g, ln1_b, wr_split)
        aff_rows = aff_em.reshape(B * N_EXPERTS, aff_em.shape[-1])

        slot = _select(aff_rows, tri, n=SEQ, col_blk=0, cap=CAP_LAT).reshape(B, N_EXPERTS, SEQ)
        y_lat = _ffn_lat(h1, per_expert(_slot_tokens(slot, CAP_LAT)), mod_all, l,
                         w_gate, w_up, w_down)
        h_lat = _scatter(h1, mod_all, l, slot, aff_em, y_lat, ln2_g, ln2_b,
                         n=SEQ, tile=TM_SCATTER, row_blk0=0, mod_row=None, cap=CAP_LAT)
        if with_ctx:
            slot_c = _select(aff_rows, tri, n=CTX_LEN, col_blk=SEQ // CTX_LEN,
                             cap=CAP_CTX).reshape(B, N_EXPERTS, CTX_LEN)
            y_ctx = _ffn_ctx(u2, per_expert(slot_c), l, w_gate, w_up, w_down)
            h_ctx = _scatter(h1, mod_all, l, slot_c, aff_em, y_ctx, ln2_g, ln2_b, n=CTX_LEN,
                             tile=CTX_LEN, row_blk0=SEQ // CTX_LEN, mod_row=B, cap=CAP_CTX)
    return h_lat
```

```python
import functools

import numpy as np
import jax
import jax.numpy as jnp
from jax import lax
from jax.experimental import pallas as pl
from jax.experimental.pallas import tpu as pltpu

D_MODEL = 1024
SEQ = 2048
DEPTH = 2
CTX_LEN = 256
GRID_W = 64
HEAD_DIM = 64
N_Q_HEADS = 4
N_KV_HEADS = 2
Q_W = 256
KV_W = 128
F_W = 256
FOURIER_GROUP_W = 64
N_BRANCH = 4
MIX_W = 3 * (Q_W + 2 * KV_W) + F_W
GATE_W = N_BRANCH * D_MODEL
WINDOW = 128
NA_KH = 8
NA_KW = 16
ROPE_THETA = 10000.0
N_EXPERTS = 16
EC_CAPACITY = 2
N_MOD = 6
DN_ALPHA = (2 * DEPTH) ** 0.25
LN_EPS = 1e-6
RMS_EPS = 1e-6
NEG_INF = -1e30
LOG2E = 1.4426950408889634

N_TOK = SEQ + CTX_LEN
TM = 256
QB = 128
NQ = 2
MIXER_KINDS = ("global", "window", "na")
TM_SCATTER = 512
N_LAT_T = SEQ // TM
N_LAT_QB = SEQ // QB
WIN_SPAN = QB + 2 * WINDOW
NA_SPAN = 640
NA_TYPES = 5
CAP_LAT = EC_CAPACITY * SEQ // N_EXPERTS
CAP_CTX = EC_CAPACITY * CTX_LEN // N_EXPERTS
R_LANES = 128

BF = jnp.bfloat16
F32 = jnp.float32
VMEM_LIMIT = 56 * 1024 * 1024


def _cparams(n_axes):
    return pltpu.CompilerParams(dimension_semantics=("arbitrary",) * n_axes,
                                vmem_limit_bytes=VMEM_LIMIT)


def _dot(a, b):
    return jnp.dot(a, b, preferred_element_type=F32)


def _dot_nt(a, b):
    return lax.dot_general(a, b, (((1,), (1,)), ((), ())), preferred_element_type=F32)


def _sigmoid(x):
    return 0.5 * jnp.tanh(0.5 * x) + 0.5


def _silu(x):
    return x * _sigmoid(x)


def _layer_norm(x, g, b):
    mu = jnp.mean(x, axis=-1, keepdims=True)
    xc = x - mu
    var = jnp.mean(xc * xc, axis=-1, keepdims=True)
    return xc * lax.rsqrt(var + LN_EPS) * g + b


def _mod_kernel(c_ref, w_ref, b_ref, o_ref):
    c = c_ref[...]
    a = c * jax.nn.sigmoid(c)
    o_ref[...] = jnp.dot(a, w_ref[...], precision=lax.Precision.HIGHEST,
                         preferred_element_type=F32) + b_ref[...]


def _modulation(c_all, w_mod, b_mod):
    L = w_mod.shape[0]
    R = c_all.shape[0]
    ncol = N_MOD * D_MODEL
    tn = D_MODEL
    return pl.pallas_call(
        _mod_kernel,
        grid=(L, ncol // tn),
        in_specs=[pl.BlockSpec((R, D_MODEL), lambda l, j: (0, 0)),
                  pl.BlockSpec((None, D_MODEL, tn), lambda l, j: (l, 0, j)),
                  pl.BlockSpec((None, 1, tn), lambda l, j: (l, 0, j))],
        out_specs=pl.BlockSpec((None, R, tn), lambda l, j: (l, 0, j)),
        out_shape=jax.ShapeDtypeStruct((L, R, ncol), F32),
        compiler_params=_cparams(2),
        name="modulation",
    )(c_all, w_mod, b_mod.reshape(L, 1, ncol))


def _rope128(x, cos, sin):
    lane = lax.broadcasted_iota(jnp.int32, x.shape, 1)
    first = (lane % 32) < 16
    up = pltpu.roll(x, 128 - 16, axis=1)
    dn = pltpu.roll(x, 16, axis=1)
    return x * cos + jnp.where(first, up, dn) * sin


def _rms128(x, gain, bd):
    x2 = x * x
    hi = x2.astype(BF)
    lo = (x2 - hi.astype(F32)).astype(BF)
    ms = _dot(hi, bd) + _dot(lo, bd)
    return x * lax.rsqrt(ms + RMS_EPS) * gain


def _q_slabs(x0, x1):
    lane = lax.broadcasted_iota(jnp.int32, x0.shape, 1)
    lo = lane < HEAD_DIM
    x0r = pltpu.roll(x0, HEAD_DIM, axis=1)
    x1r = pltpu.roll(x1, HEAD_DIM, axis=1)
    scale = HEAD_DIM ** -0.5 * LOG2E
    slabs = [jnp.where(lo, x0, 0.0), jnp.where(lo, x0r, 0.0),
             jnp.where(lo, 0.0, x1r), jnp.where(lo, 0.0, x1)]
    return jnp.concatenate([(s * scale).astype(BF) for s in slabs], axis=1)


def _proj_kernel(hl_ref, hc_ref, mod_ref, w_ref, gq_ref, gk_ref, cos_ref, sin_ref, bd_ref,
                 q_ref, kv_ref):
    m = mod_ref[...]
    h = jnp.where(pl.program_id(1) == N_LAT_T, hc_ref[...], hl_ref[...])
    u = h * (1.0 + m[1:2]) + m[0:1]
    p = _dot(u.astype(BF), w_ref[...])
    cos = cos_ref[...]
    sin = sin_ref[...]
    bd = bd_ref[...]
    gq = gq_ref[...]
    gk = gk_ref[...]
    c0, s0 = cos[:, :128], sin[:, :128]
    c1, s1 = cos[:, 128:], sin[:, 128:]

    qa0 = _rope128(_rms128(p[:, 0:128], gq, bd), c0, s0)
    qa1 = _rope128(_rms128(p[:, 128:256], gq, bd), c1, s1)
    q_ref[:, 0:512] = _q_slabs(qa0, qa1)
    q_ref[:, 512:1024] = _q_slabs(_rope128(p[:, 768:896], c0, s0), _rope128(p[:, 896:1024], c1, s1))
    q_ref[:, 1024:1536] = _q_slabs(p[:, 1280:1408], p[:, 1408:1536])
    kv_ref[:, 0:128] = _rope128(_rms128(p[:, 256:384], gk, bd), c0, s0).astype(BF)
    kv_ref[:, 128:256] = p[:, 384:512].astype(BF)
    kv_ref[:, 256:384] = _rope128(p[:, 1024:1152], c0, s0).astype(BF)
    kv_ref[:, 384:512] = p[:, 1152:1280].astype(BF)
    kv_ref[:, 512:768] = p[:, 1536:1792].astype(BF)
    kv_ref[:, 768:1024] = p[:, 512:768].astype(BF)


QS_W = 2 * Q_W
KV_COL = {"global": (0, 1), "window": (2, 3), "na": (4, 5)}
Q_COL = {"global": 0, "window": 1, "na": 2}
ACT_W = 1024
FB_COL = 3


def _lat_ctx_specs(h_lat, h_ctx):
    return ([pl.BlockSpec((None, TM, D_MODEL), lambda b, t: (b, jnp.minimum(t, N_LAT_T - 1), 0)),
             pl.BlockSpec((None, TM, D_MODEL), lambda b, t: (b, 0, 0))], [h_lat, h_ctx])


def _project(h_lat, h_ctx, mod, l, w_mix, gq, gk, cos_t, sin_t, bd):
    B = h_lat.shape[0]
    n_t = N_TOK // TM
    ctx_row = B
    tok = lambda w: pl.BlockSpec((None, TM, w), lambda b, t: (b, t, 0))
    const = lambda shape: pl.BlockSpec(shape, lambda b, t: (0,) * len(shape))
    widths = (3 * QS_W, ACT_W)
    h_specs, h_args = _lat_ctx_specs(h_lat, h_ctx)
    return pl.pallas_call(
        _proj_kernel,
        grid=(B, n_t),
        in_specs=h_specs + [
            pl.BlockSpec((None, None, N_MOD, D_MODEL),
                         lambda b, t: (l, jnp.where(t == N_LAT_T, ctx_row, b), 0, 0)),
            pl.BlockSpec((None, D_MODEL, MIX_W), lambda b, t: (l, 0, 0)),
            const((1, 128)), const((1, 128)),
            pl.BlockSpec((TM, Q_W), lambda b, t: (t, 0)),
            pl.BlockSpec((TM, Q_W), lambda b, t: (t, 0)),
            const((128, 128))],
        out_specs=[tok(w) for w in widths],
        out_shape=[jax.ShapeDtypeStruct((B, N_TOK, w), BF) for w in widths],
        compiler_params=_cparams(2),
        name="mixer_projection",
    )(*h_args, mod, w_mix, gq, gk, cos_t, sin_t, bd)


def _fourier_kernel(x_ref, ce_ref, co_ref, cc_ref, sc_ref, o_ref, zp_ref, zm_ref, *, row0, n):
    half = n // 2

    @pl.when(pl.program_id(1) == 0)
    def _():
        x = x_ref[row0:row0 + n, :]
        for part, w_ref in enumerate((cc_ref, sc_ref)):
            z = _dot(x, w_ref[...])
            z = z if part == 0 else -z
            rows = slice(part * half, (part + 1) * half)
            zp_ref[rows, :] = (z[:half] + z[half:]).astype(BF)
            zm_ref[rows, :] = (z[:half] - z[half:]).astype(BF)

    scale = (n * FOURIER_GROUP_W) ** -0.5
    o_ref[:, 0:F_W] = (_dot(ce_ref[...], zp_ref[...]) * scale).astype(BF)
    o_ref[:, F_W:2 * F_W] = (_dot(co_ref[...], zm_ref[...]) * scale).astype(BF)


def _fourier(fb_all, cs, cc, sc, *, row0, n, tq):
    B = fb_all.shape[0]
    half = n // 2
    out = pl.pallas_call(
        functools.partial(_fourier_kernel, row0=row0, n=n),
        grid=(B, half // tq),
        in_specs=[pl.BlockSpec((None, N_TOK, F_W), lambda b, t: (b, 0, FB_COL)),
                  pl.BlockSpec((None, tq, n), lambda b, t: (0, t, 0)),
                  pl.BlockSpec((None, tq, n), lambda b, t: (1, t, 0)),
                  pl.BlockSpec((F_W, F_W), lambda b, t: (0, 0)),
                  pl.BlockSpec((F_W, F_W), lambda b, t: (0, 0))],
        out_specs=pl.BlockSpec((None, tq, 2 * F_W), lambda b, t: (b, t, 0)),
        out_shape=jax.ShapeDtypeStruct((B, half, 2 * F_W), BF),
        scratch_shapes=[pltpu.VMEM((n, F_W), BF), pltpu.VMEM((n, F_W), BF)],
        compiler_params=_cparams(2),
        name="fourier_mixer",
    )(fb_all, cs, cs, cc, sc)
    return out.reshape(B, n, F_W)


def _attend(q, kcat, vcat, add_bias, sinks):
    outs = []
    for h in range(N_KV_HEADS):
        qs = jnp.concatenate([q[:, (2 * h) * 128:(2 * h + 1) * 128],
                              q[:, (2 * h + 1) * 128:(2 * h + 2) * 128]], axis=0)
        s = add_bias(_dot_nt(qs, kcat), h)
        m = jnp.max(s, axis=-1, keepdims=True)
        if sinks is not None:
            row = lax.broadcasted_iota(jnp.int32, (2 * QB, 1), 0)
            sk = jnp.where(row < QB, sinks[2 * h], sinks[2 * h + 1])
            m = jnp.maximum(m, sk)
        p = jnp.exp2(s - m)
        l = jnp.sum(p, axis=-1, keepdims=True)
        if sinks is not None:
            l = l + jnp.exp2(sk - m)
        outs.append(_dot(p.astype(BF), vcat) / l)
    o0, o1 = outs
    lane = lax.broadcasted_iota(jnp.int32, (QB, 128), 1)
    lo = lane < HEAD_DIM
    left = jnp.where(lo, o0[:QB], pltpu.roll(o0[QB:], HEAD_DIM, axis=1))
    right = jnp.where(lo, pltpu.roll(o1[:QB], HEAD_DIM, axis=1), o1[QB:])
    return jnp.concatenate([left, right], axis=1)


def _mixers_kernel(*refs, kinds, with_ctx):
    refs = list(refs)
    sinks = None
    if "window" in kinds:
        sink_ref = refs.pop(0)
        sinks = [sink_ref[i] * LOG2E for i in range(N_Q_HEADS)]
    io = {}
    for kind in kinds:
        io[kind] = [refs.pop(0), refs.pop(0), refs.pop(0)]
    if "na" in kinds:
        tab_ref = refs.pop(0)
    out_ref = refs.pop(0)
    for n, kind in enumerate(kinds):
        io[kind].append(out_ref.at[:, n * Q_W:(n + 1) * Q_W])
    step = pl.program_id(1)
    no_bias = lambda s, h: s

    def lat_block(kind, j, q, k_ref, v_ref):
        ctx_k, ctx_v = k_ref[SEQ:N_TOK, :], v_ref[SEQ:N_TOK, :]
        if kind == "global":
            return _attend(q, k_ref[...], v_ref[...], no_bias, None)
        if kind == "window":
            start = pl.multiple_of(QB * jnp.clip(j - 1, 0, N_LAT_QB - 3), QB)
            kcat = jnp.concatenate([k_ref[pl.ds(start, WIN_SPAN), :], ctx_k], axis=0)
            vcat = jnp.concatenate([v_ref[pl.ds(start, WIN_SPAN), :], ctx_v], axis=0)
            nk = WIN_SPAN + CTX_LEN
            col = lax.broadcasted_iota(jnp.int32, (2 * QB, nk), 1)
            row = lax.broadcasted_iota(jnp.int32, (2 * QB, nk), 0)
            q_pos = QB * j + row % QB
            ok = (col >= WIN_SPAN) | (jnp.abs(q_pos - (start + col)) <= WINDOW)
            return _attend(q, kcat, vcat, lambda s, h: jnp.where(ok, s, NEG_INF), sinks)
        start = pl.multiple_of(QB * jnp.clip(j - 2, 0, N_LAT_QB - 5), QB)
        kcat = jnp.concatenate([k_ref[pl.ds(start, NA_SPAN), :], ctx_k], axis=0)
        vcat = jnp.concatenate([v_ref[pl.ds(start, NA_SPAN), :], ctx_v], axis=0)
        tix = jnp.where(j < 2, j, jnp.where(j > N_LAT_QB - 3, j - (N_LAT_QB - 5), 2))

        def na_bias(s, h):
            b = tab_ref[tix, 2 * h:2 * h + 2]
            return s + b.reshape(2 * QB, NA_SPAN + CTX_LEN)

        return _attend(q, kcat, vcat, na_bias, None)

    def latent():
        for i in range(NQ):
            rows = slice(i * QB, (i + 1) * QB)
            for kind in kinds:
                q_ref, k_ref, v_ref, o_ref = io[kind]
                o_ref[rows, :] = lat_block(kind, step * NQ + i, q_ref[rows, :], k_ref, v_ref).astype(BF)

    def context():
        for i in range(NQ):
            rows = slice(i * QB, (i + 1) * QB)
            for kind in kinds:
                q_ref, k_ref, v_ref, o_ref = io[kind]
                o = _attend(q_ref[rows, :], k_ref[SEQ:N_TOK, :], v_ref[SEQ:N_TOK, :], no_bias,
                            sinks if kind == "window" else None)
                o_ref[rows, :] = o.astype(BF)

    if with_ctx:
        pl.when(step < N_LAT_QB // NQ)(latent)
        pl.when(step >= N_LAT_QB // NQ)(context)
    else:
        latent()


def _mixers(kinds, q_all, kv_all, sink, na_table, with_ctx):
    B = q_all.shape[0]
    n_steps = (N_LAT_QB + (CTX_LEN // QB if with_ctx else 0)) // NQ
    in_specs, args = [], []
    if "window" in kinds:
        in_specs.append(pl.BlockSpec(memory_space=pltpu.SMEM))
        args.append(sink)
    for kind in kinds:
        qc, (kc, vc) = Q_COL[kind], KV_COL[kind]
        in_specs += [pl.BlockSpec((None, NQ * QB, QS_W), lambda b, j, c=qc: (b, j, c)),
                     pl.BlockSpec((None, N_TOK, KV_W), lambda b, j, c=kc: (b, 0, c)),
                     pl.BlockSpec((None, N_TOK, KV_W), lambda b, j, c=vc: (b, 0, c))]
        args += [q_all, kv_all, kv_all]
    if "na" in kinds:
        in_specs.append(pl.BlockSpec(na_table.shape, lambda b, j: (0, 0, 0, 0)))
        args.append(na_table)
    width = Q_W * len(kinds)
    return pl.pallas_call(
        functools.partial(_mixers_kernel, kinds=kinds, with_ctx=with_ctx),
        grid=(B, n_steps),
        in_specs=in_specs,
        out_specs=pl.BlockSpec((None, NQ * QB, width), lambda b, j: (b, j, 0)),
        out_shape=jax.ShapeDtypeStruct((B, n_steps * NQ * QB, width), BF),
        compiler_params=_cparams(2),
        name="attention_" + "_".join(kinds),
    )(*args)


def _merge_kernel(*refs, with_ctx):
    if with_ctx:
        (h_ref, hc_ref, mod_ref, acd_ref, bl_ref, bc_ref,
         wg_ref, wb_ref, wo_ref, g_ref, b_ref, wr_ref, h1_ref, aff_ref, u_ref) = refs
    else:
        (h_ref, mod_ref, acd_ref, bl_ref,
         wg_ref, wb_ref, wo_ref, g_ref, b_ref, wr_ref, h1_ref, aff_ref) = refs
        u_ref = None
    m = mod_ref[...]
    h = h_ref[...]
    br_b = bl_ref[...]
    if with_ctx:
        is_ctx = pl.program_id(1) == N_LAT_T
        h = jnp.where(is_ctx, hc_ref[...], h)
        br_b = jnp.where(is_ctx, bc_ref[...], br_b)
    u = (h * (1.0 + m[1:2]) + m[0:1]).astype(BF)
    acd = acd_ref[...]
    branches = (acd[:, 0:Q_W], br_b, acd[:, Q_W:2 * Q_W], acd[:, 2 * Q_W:3 * Q_W])
    merged = None
    for i in range(N_BRANCH):
        gate = _sigmoid(_dot(u, wg_ref[:, i * D_MODEL:(i + 1) * D_MODEL]))
        term = gate * _dot(branches[i], wb_ref[i])
        merged = term if merged is None else merged + term
    y = _dot(merged.astype(BF), wo_ref[...])
    h1 = _layer_norm(DN_ALPHA * h + m[2:3] * y, g_ref[...], b_ref[...])
    h1_ref[...] = h1

    u2 = h1 * (1.0 + m[4:5]) + m[3:4]
    u_hi = u2.astype(BF)
    if u_ref is not None:
        u_ref[...] = u_hi
    u_lo = (u2 - u_hi.astype(F32)).astype(BF)
    part = _dot(u_hi, wr_ref[...]) + _dot(u_lo, wr_ref[...])
    logits = part + pltpu.roll(part, R_LANES - N_EXPERTS, axis=1)
    lane = lax.broadcasted_iota(jnp.int32, logits.shape, 1)
    lg = jnp.where(lane < N_EXPERTS, logits, NEG_INF)
    ex = jnp.exp(lg - jnp.max(lg, axis=-1, keepdims=True))
    aff = ex / jnp.sum(ex, axis=-1, keepdims=True)
    aff_ref[...] = aff.T[0:N_EXPERTS]


def _merge(h_lat, h_ctx, mod, l, br_acd, brb_lat, brb_ctx, w_gate, w_branch, w_out, ln_g, ln_b, w_router_split):
    B = h_lat.shape[0]
    with_ctx = brb_ctx is not None
    n_t = N_LAT_T + (1 if with_ctx else 0)
    tok = lambda w: pl.BlockSpec((None, TM, w), lambda b, t: (b, t, 0))
    layer = lambda shape: pl.BlockSpec((None,) + shape, lambda b, t: (l,) + (0,) * len(shape))
    if with_ctx:
        in_specs, args = _lat_ctx_specs(h_lat, h_ctx)
    else:
        in_specs, args = [tok(D_MODEL)], [h_lat]
    in_specs += [pl.BlockSpec((None, None, N_MOD, D_MODEL),
                              lambda b, t: (l, jnp.where(t == N_LAT_T, B, b), 0, 0)),
                 tok(3 * Q_W),
                 pl.BlockSpec((None, TM, F_W), lambda b, t: (b, jnp.minimum(t, N_LAT_T - 1), 0))]
    args += [mod, br_acd, brb_lat]
    if with_ctx:
        in_specs.append(pl.BlockSpec((None, TM, F_W), lambda b, t: (b, 0, 0)))
        args.append(brb_ctx)
    in_specs += [layer((D_MODEL, GATE_W)), layer((N_BRANCH, Q_W, D_MODEL)),
                 layer((D_MODEL, D_MODEL)), layer((1, D_MODEL)), layer((1, D_MODEL)),
                 layer((D_MODEL, R_LANES))]
    args += [w_gate, w_branch, w_out, ln_g, ln_b, w_router_split]
    rows = n_t * TM
    out_specs = [tok(D_MODEL), pl.BlockSpec((None, N_EXPERTS, TM), lambda b, t: (b, 0, t))]
    out_shape = [jax.ShapeDtypeStruct((B, rows, D_MODEL), F32),
                 jax.ShapeDtypeStruct((B, N_EXPERTS, rows), F32)]
    if with_ctx:
        out_specs.append(tok(D_MODEL))
        out_shape.append(jax.ShapeDtypeStruct((B, rows, D_MODEL), BF))
    out = pl.pallas_call(
        functools.partial(_merge_kernel, with_ctx=with_ctx),
        grid=(B, n_t),
        in_specs=in_specs,
        out_specs=out_specs,
        out_shape=out_shape,
        compiler_params=_cparams(2),
        name="gated_merge",
    )(*args)
    return out if with_ctx else (out[0], out[1], None)


def _excl_cumsum_lanes(x, tri):
    R, n = x.shape
    nc = n // 128
    xs = jnp.concatenate([x[:, c * 128:(c + 1) * 128] for c in range(nc)], axis=0)
    incl = _dot(xs.astype(BF), tri)
    excl = incl - xs
    tot = incl[:, 127:128]
    off = jnp.zeros((R, 1), F32)
    pieces = []
    for c in range(nc):
        pieces.append(excl[c * R:(c + 1) * R] + off)
        off = off + tot[c * R:(c + 1) * R]
    return jnp.concatenate(pieces, axis=1)


MIN_NORMAL = 2.0 ** -126
MANTISSA_STEPS = 30


def _select_top(a, cap, tri):
    def count_ge(c):
        return jnp.sum(jnp.where(a >= c, 1.0, 0.0), axis=-1, keepdims=True)

    has = count_ge(MIN_NORMAL) >= cap
    p = jnp.full((a.shape[0], 1), MIN_NORMAL, F32)
    for s in (64, 32, 16, 8, 4, 2, 1):
        cand = p * (2.0 ** s)
        p = jnp.where(count_ge(cand) >= cap, cand, p)
    lo = jnp.where(has, p, 0.0)
    hi = jnp.where(has, 2.0 * p, MIN_NORMAL)
    for _ in range(MANTISSA_STEPS):
        mid = 0.5 * (lo + hi)
        ok = count_ge(mid) >= cap
        lo = jnp.where(ok, mid, lo)
        hi = jnp.where(ok, hi, mid)
    above = a >= hi
    edge = jnp.where(a >= lo, jnp.where(above, 0.0, 1.0), 0.0)
    need = cap - jnp.sum(jnp.where(above, 1.0, 0.0), axis=-1, keepdims=True)
    edge_rank = _excl_cumsum_lanes(edge, tri)
    sel = jnp.where(above, 1.0, jnp.where(edge_rank < need, edge, 0.0))
    return sel > 0.5, _excl_cumsum_lanes(sel, tri)


def _select_kernel(aff_ref, tri_ref, slot_ref, *, cap):
    sel, rank = _select_top(aff_ref[...], cap, tri_ref[...])
    slot_ref[...] = jnp.where(sel, rank, -1.0).astype(jnp.int32)


def _select(aff_rows, tri, *, n, col_blk, cap):
    R = aff_rows.shape[0]
    return pl.pallas_call(
        functools.partial(_select_kernel, cap=cap),
        grid=(1,),
        in_specs=[pl.BlockSpec((R, n), lambda i: (0, col_blk)),
                  pl.BlockSpec((128, 128), lambda i: (0, 0))],
        out_specs=pl.BlockSpec((R, n), lambda i: (0, 0)),
        out_shape=jax.ShapeDtypeStruct((R, n), jnp.int32),
        compiler_params=_cparams(1),
        name="ec_select",
    )(aff_rows, tri)


def _swiglu(xs, wg, wu, wd):
    a = _dot(xs, wg)
    up = _dot(xs, wu)
    hm = _silu(a) * up
    return _dot(hm.astype(BF), wd)


def _index_kernel(slot_ref, idx_ref):
    n = slot_ref.shape[1]
    cap = idx_ref.shape[1]
    slot_tm = _token_major(slot_ref[...].astype(F32)).astype(jnp.int32)
    tok = lax.broadcasted_iota(jnp.int32, (n, cap), 0).astype(F32)
    lane = lax.broadcasted_iota(jnp.int32, (n, cap), 1)
    rows = [jnp.sum(jnp.where(slot_tm[:, e:e + 1] == lane, tok, 0.0), axis=0, keepdims=True)
            for e in range(N_EXPERTS)]
    idx_ref[...] = jnp.concatenate(rows, axis=0).astype(jnp.int32)


def _slot_tokens(slot, cap):
    B, _, n = slot.shape
    return pl.pallas_call(
        _index_kernel,
        grid=(B,),
        in_specs=[pl.BlockSpec((None, N_EXPERTS, n), lambda b: (b, 0, 0))],
        out_specs=pl.BlockSpec((None, N_EXPERTS, cap), lambda b: (b, 0, 0)),
        out_shape=jax.ShapeDtypeStruct((B, N_EXPERTS, cap), jnp.int32),
        compiler_params=_cparams(1),
        name="ec_slot_tokens",
    )(slot)


FFN_CHUNKS = 4


def _gather_copy(h_hbm, xbuf, sem, b, tok, buf, j):
    return pltpu.make_async_copy(h_hbm.at[b, pl.ds(tok, 1), :], xbuf.at[buf, pl.ds(j, 1), :],
                                 sem.at[buf])


GATHER_AHEAD = 2
N_GATHER_BUF = GATHER_AHEAD + 1


def _ffn_lat_kernel(idx0_ref, idx1_ref, idx2_ref, h_hbm, mod_ref, wg_ref, wu_ref, wd_ref, y_ref,
                    wg_s, wu_s, wd_s, xbuf, sem):
    e, b = pl.program_id(0), pl.program_id(1)
    n_b = pl.num_programs(1)
    step = e * n_b + b
    buf = step % N_GATHER_BUF

    def start_rows(idx, ahead, lo=0, hi=CAP_LAT):
        bb, into = (b + ahead) % n_b, (step + ahead) % N_GATHER_BUF
        for j in range(lo, hi):
            _gather_copy(h_hbm, xbuf, sem, bb, idx[0, j], into, j).start(priority=j % 2)

    def wait_rows(which):
        for j in range(CAP_LAT):
            _gather_copy(h_hbm, xbuf, sem, b, 0, which, j).wait()

    @pl.when(step == 0)
    def _():
        start_rows(idx0_ref, 0)
        start_rows(idx1_ref, 1)

    @pl.when(b == 0)
    def _():
        wg_s[...] = wg_ref[...].astype(BF)
        wu_s[...] = wu_ref[...].astype(BF)
        wd_s[...] = wd_ref[...].astype(BF)

    wait_rows(buf)
    m = mod_ref[...]
    xs = (xbuf[buf] * (1.0 + m[4:5]) + m[3:4]).astype(BF)
    fc = wg_s.shape[1] // FFN_CHUNKS
    per = CAP_LAT // FFN_CHUNKS
    y = None
    for c in range(FFN_CHUNKS):
        start_rows(idx2_ref, GATHER_AHEAD, c * per, (c + 1) * per)
        a = _dot(xs, wg_s[:, c * fc:(c + 1) * fc])
        up = _dot(xs, wu_s[:, c * fc:(c + 1) * fc])
        hm = (_silu(a) * up).astype(BF)
        part = _dot(hm, wd_s[c * fc:(c + 1) * fc, :])
        y = part if y is None else y + part
    y_ref[...] = y.astype(BF)

    @pl.when(step + 1 == pl.num_programs(0) * n_b)
    def _():
        for ahead in range(1, N_GATHER_BUF):
            wait_rows((step + ahead) % N_GATHER_BUF)


def _onehot_rows(slot, u, cap):
    n = slot.shape[1]
    hit = slot == lax.broadcasted_iota(jnp.int32, (cap, n), 0)
    return _dot(jnp.where(hit, 1.0, 0.0).astype(BF), u).astype(BF)


def _ffn_ctx_kernel(u_ref, slot_ref, wg_ref, wu_ref, wd_ref, y_ref):
    B = u_ref.shape[0]
    xs = jnp.concatenate([_onehot_rows(slot_ref[b], u_ref[b], CAP_CTX) for b in range(B)], axis=0)
    y = _swiglu(xs, wg_ref[...].astype(BF), wu_ref[...].astype(BF), wd_ref[...].astype(BF))
    for b in range(B):
        y_ref[b] = y[b * CAP_CTX:(b + 1) * CAP_CTX].astype(BF)


def _ffn_lat(h1, idx, mod, l, wg, wu, wd):
    B = h1.shape[0]
    F = wg.shape[-1]
    wspec = lambda s: pl.BlockSpec((None, None) + s, lambda e, b: (l, e, 0, 0))

    def idx_spec(ahead):
        def index_map(e, b):
            return (b + ahead) % B, jnp.minimum(e + (b + ahead) // B, N_EXPERTS - 1), 0, 0
        return pl.BlockSpec((None, None, 1, CAP_LAT), index_map, memory_space=pltpu.SMEM)

    return pl.pallas_call(
        _ffn_lat_kernel,
        grid=(N_EXPERTS, B),
        in_specs=[idx_spec(0), idx_spec(1), idx_spec(GATHER_AHEAD),
                  pl.BlockSpec(memory_space=pl.ANY),
                  pl.BlockSpec((None, None, N_MOD, D_MODEL), lambda e, b: (l, b, 0, 0)),
                  wspec((D_MODEL, F)), wspec((D_MODEL, F)), wspec((F, D_MODEL))],
        out_specs=pl.BlockSpec((None, CAP_LAT, D_MODEL), lambda e, b: (b, e, 0)),
        out_shape=jax.ShapeDtypeStruct((B, N_EXPERTS * CAP_LAT, D_MODEL), BF),
        scratch_shapes=[pltpu.VMEM((D_MODEL, F), BF), pltpu.VMEM((D_MODEL, F), BF),
                        pltpu.VMEM((F, D_MODEL), BF),
                        pltpu.VMEM((N_GATHER_BUF, CAP_LAT, D_MODEL), F32),
                        pltpu.SemaphoreType.DMA((N_GATHER_BUF,))],
        compiler_params=_cparams(2),
        name="ec_ffn_latent",
    )(idx, idx, idx, h1, mod, wg, wu, wd)


def _ffn_ctx(u, slot_em, l, wg, wu, wd):
    B = u.shape[0]
    F = wg.shape[-1]
    wspec = lambda s: pl.BlockSpec((None, None) + s, lambda e: (l, e, 0, 0))
    return pl.pallas_call(
        _ffn_ctx_kernel,
        grid=(N_EXPERTS,),
        in_specs=[pl.BlockSpec((B, CTX_LEN, D_MODEL), lambda e: (0, SEQ // CTX_LEN, 0),
                               pipeline_mode=pl.Buffered(1)),
                  pl.BlockSpec((B, None, 1, CTX_LEN), lambda e: (0, e, 0, 0)),
                  wspec((D_MODEL, F)), wspec((D_MODEL, F)), wspec((F, D_MODEL))],
        out_specs=pl.BlockSpec((B, CAP_CTX, D_MODEL), lambda e: (0, e, 0)),
        out_shape=jax.ShapeDtypeStruct((B, N_EXPERTS * CAP_CTX, D_MODEL), BF),
        compiler_params=_cparams(1),
        name="ec_ffn_context",
    )(u, slot_em, wg, wu, wd)


def _token_major(x_em):
    pad = jnp.full((R_LANES - N_EXPERTS, x_em.shape[1]), -1.0, F32)
    return jnp.concatenate([x_em, pad], axis=0).T


def _scatter_matrix(slot_tm, aff_tm, cap):
    rows = slot_tm.shape[0]
    if cap % 128 == 0:
        lane = lax.broadcasted_iota(jnp.int32, (rows, cap), 1)
        pieces = [jnp.where(slot_tm[:, e:e + 1] == lane, aff_tm[:, e:e + 1], 0.0).astype(BF)
                  for e in range(N_EXPERTS)]
        return jnp.concatenate(pieces, axis=1)
    lane = lax.broadcasted_iota(jnp.int32, (rows, N_EXPERTS * cap), 1)
    acc = jnp.zeros((rows, N_EXPERTS * cap), F32)
    for e in range(N_EXPERTS):
        s = slot_tm[:, e:e + 1]
        target = jnp.where(s >= 0, s + e * cap, -1)
        acc = acc + jnp.where(target == lane, aff_tm[:, e:e + 1], 0.0)
    return acc.astype(BF)


def _scatter_kernel(h_ref, mod_ref, slot_ref, aff_ref, y_ref, g_ref, b_ref, o_ref, *, cap):
    slot_tm = _token_major(slot_ref[...].astype(F32)).astype(jnp.int32)
    f = _dot(_scatter_matrix(slot_tm, _token_major(aff_ref[...]), cap), y_ref[...])
    m = mod_ref[...]
    o_ref[...] = _layer_norm(DN_ALPHA * h_ref[...] + m[5:6] * f, g_ref[...], b_ref[...])


def _scatter(h1, mod, l, slot, aff_em, y, ln_g, ln_b, *, n, tile, row_blk0, mod_row, cap):
    B = h1.shape[0]
    mrow = (lambda b: b) if mod_row is None else (lambda b: mod_row)
    return pl.pallas_call(
        functools.partial(_scatter_kernel, cap=cap),
        grid=(B, n // tile),
        in_specs=[pl.BlockSpec((None, tile, D_MODEL), lambda b, t: (b, row_blk0 + t, 0)),
                  pl.BlockSpec((None, None, N_MOD, D_MODEL), lambda b, t: (l, mrow(b), 0, 0)),
                  pl.BlockSpec((None, N_EXPERTS, tile), lambda b, t: (b, 0, t)),
                  pl.BlockSpec((None, N_EXPERTS, tile), lambda b, t: (b, 0, row_blk0 + t)),
                  pl.BlockSpec((None, N_EXPERTS * cap, D_MODEL), lambda b, t: (b, 0, 0)),
                  pl.BlockSpec((None, 1, D_MODEL), lambda b, t: (l, 0, 0)),
                  pl.BlockSpec((None, 1, D_MODEL), lambda b, t: (l, 0, 0))],
        out_specs=pl.BlockSpec((None, tile, D_MODEL), lambda b, t: (b, t, 0)),
        out_shape=jax.ShapeDtypeStruct((B, n, D_MODEL), F32),
        compiler_params=_cparams(2),
        name="ec_scatter_norm",
    )(h1, mod, slot, aff_em, y, ln_g, ln_b)


def _rope_tables():
    t = jnp.arange(SEQ)
    axis_dim = HEAD_DIM // 2
    inv = ROPE_THETA ** (-jnp.arange(0, axis_dim, 2, dtype=F32) / axis_dim)
    ang_r = (t // GRID_W).astype(F32)[:, None] * inv
    ang_c = (t % GRID_W).astype(F32)[:, None] * inv
    cr, sr, cc, sc = jnp.cos(ang_r), jnp.sin(ang_r), jnp.cos(ang_c), jnp.sin(ang_c)
    cos_h = jnp.concatenate([cr, cr, cc, cc], axis=-1)
    sin_h = jnp.concatenate([-sr, sr, -sc, sc], axis=-1)
    cos_t = jnp.concatenate([jnp.tile(cos_h, (1, N_Q_HEADS)), jnp.ones((CTX_LEN, Q_W), F32)], axis=0)
    sin_t = jnp.concatenate([jnp.tile(sin_h, (1, N_Q_HEADS)), jnp.zeros((CTX_LEN, Q_W), F32)], axis=0)
    return cos_t, sin_t


def _dft_tables(n):
    s = np.arange(n // 2, dtype=np.int64)
    k = 2 * s[None, :, None] + np.arange(2, dtype=np.int64)[:, None, None]
    ang = 2.0 * np.pi * ((k * s[None, None, :]) % n).astype(np.float64) / n
    cs = np.concatenate([np.cos(ang), np.sin(ang)], axis=-1).astype(np.float32)
    c = np.arange(F_W)
    same = (c[:, None] // FOURIER_GROUP_W) == (c[None, :] // FOURIER_GROUP_W)
    angc = 2.0 * np.pi * (((c[:, None] % FOURIER_GROUP_W) * (c[None, :] % FOURIER_GROUP_W))
                          % FOURIER_GROUP_W) / FOURIER_GROUP_W
    cc = np.where(same, np.cos(angc), 0.0).astype(np.float32)
    sc = np.where(same, np.sin(angc), 0.0).astype(np.float32)
    return cs, cc, sc


def _na_bias_table(rpb):
    rows = SEQ // GRID_W
    q_rows, k_rows = QB // GRID_W, NA_SPAN // GRID_W
    n_dc = GRID_W - NA_KW
    padded = jnp.pad(rpb.astype(F32) * LOG2E, ((0, 0), (0, 0), (n_dc, n_dc)))
    toe = jnp.stack([padded[:, :, GRID_W - 1 - c:2 * GRID_W - 1 - c] for c in range(GRID_W)], axis=2)
    zero_blk = jnp.zeros((N_Q_HEADS, GRID_W, GRID_W), F32)
    col = np.arange(GRID_W)
    c0 = np.clip(col - NA_KW // 2, 0, GRID_W - NA_KW)
    col_ok = (col[None, :] >= c0[:, None]) & (col[None, :] < c0[:, None] + NA_KW)
    tables = []
    for j in (0, 1, 2, N_LAT_QB - 2, N_LAT_QB - 1):
        kb = min(max(j - 2, 0), N_LAT_QB - 5)
        q_parts = []
        for qr in range(q_rows):
            r = q_rows * j + qr
            r0 = min(max(r - NA_KH // 2, 0), rows - NA_KH)
            k_parts = []
            for ki in range(k_rows):
                kr = q_rows * kb + ki
                if r0 <= kr < r0 + NA_KH:
                    k_parts.append(jnp.where(col_ok[None], toe[:, kr - r + NA_KH - 1], NEG_INF))
                else:
                    k_parts.append(zero_blk + NEG_INF)
            q_parts.append(jnp.concatenate(k_parts, axis=-1))
        tables.append(jnp.concatenate(q_parts, axis=-2))
    tab = jnp.stack(tables, axis=0)
    return jnp.concatenate([tab, jnp.zeros(tab.shape[:3] + (CTX_LEN,), F32)], axis=-1)


def kernel(x, c, ctx, c_ctx, w_mod, b_mod, w_in, qk_gain, sink_logit, na_rpb, w_branch, w_out,
           ln1_g, ln1_b, w_router, w_gate, w_up, w_down, ln2_g, ln2_b):
    B = x.shape[0]
    L = w_mod.shape[0]
    assert x.shape[1:] == (SEQ, D_MODEL) and ctx.shape[1:] == (CTX_LEN, D_MODEL)

    n_rows = -(-(B + 1) // 8) * 8
    c_all = jnp.concatenate([c, c_ctx[None, :], jnp.zeros((n_rows - B - 1, D_MODEL), F32)], axis=0)
    mod_all = _modulation(c_all, w_mod, b_mod).reshape(L, n_rows, N_MOD, D_MODEL)

    cos_t, sin_t = _rope_tables()
    gidx = np.arange(128)
    bd = jnp.asarray(np.where((gidx[:, None] // HEAD_DIM) == (gidx[None, :] // HEAD_DIM),
                              1.0 / HEAD_DIM, 0.0), BF)
    tri = jnp.asarray(gidx[:, None] <= gidx[None, :], BF)
    cs_lat, cc, sc = _dft_tables(SEQ)
    cs_lat, cc, sc = jnp.asarray(cs_lat).astype(BF), jnp.asarray(cc).astype(BF), jnp.asarray(sc).astype(BF)
    cs_ctx = jnp.asarray(_dft_tables(CTX_LEN)[0]).astype(BF)

    w_mix_bf = w_in[:, :, :MIX_W].astype(BF)
    w_gatecols_bf = w_in[:, :, MIX_W:].astype(BF)
    w_branch_bf = w_branch.astype(BF)
    w_out_bf = w_out.astype(BF)
    wr_hi = w_router.astype(BF)
    wr_lo = (w_router - wr_hi.astype(F32)).astype(BF)
    wr_split = jnp.concatenate(
        [wr_hi, wr_lo, jnp.zeros((L, D_MODEL, R_LANES - 2 * N_EXPERTS), BF)], axis=-1)
    ln1_g, ln1_b, ln2_g, ln2_b = (a[:, None, :] for a in (ln1_g, ln1_b, ln2_g, ln2_b))
    per_expert = lambda a: a.reshape(a.shape[0], N_EXPERTS, 1, a.shape[-1])

    h_lat, h_ctx = x, ctx
    for l in range(L):
        with_ctx = l < L - 1
        gq = jnp.tile(qk_gain[l, 0], 2)[None, :]
        gk = jnp.tile(qk_gain[l, 1], 2)[None, :]
        q_all, kv_all = _project(h_lat, h_ctx, mod_all, l, w_mix_bf, gq, gk, cos_t, sin_t, bd)

        br_acd = _mixers(MIXER_KINDS, q_all, kv_all, sink_logit[l], _na_bias_table(na_rpb[l]), with_ctx)
        brb_lat = _fourier(kv_all, cs_lat, cc, sc, row0=0, n=SEQ, tq=512)
        brb_ctx = (_fourier(kv_all, cs_ctx, cc, sc, row0=SEQ, n=CTX_LEN, tq=CTX_LEN // 2)
                   if with_ctx else None)

        h1, aff_em, u2 = _merge(h_lat, h_ctx, mod_all, l, br_acd, brb_lat, brb_ctx,
                                w_gatecols_bf, w_branch_bf, w_out_bf, ln1_g, ln1_b, wr_split)
        aff_rows = aff_em.reshape(B * N_EXPERTS, aff_em.shape[-1])

        slot = _select(aff_rows, tri, n=SEQ, col_blk=0, cap=CAP_LAT).reshape(B, N_EXPERTS, SEQ)
        y_lat = _ffn_lat(h1, per_expert(_slot_tokens(slot, CAP_LAT)), mod_all, l,
                         w_gate, w_up, w_down)
        h_lat = _scatter(h1, mod_all, l, slot, aff_em, y_lat, ln2_g, ln2_b,
                         n=SEQ, tile=TM_SCATTER, row_blk0=0, mod_row=None, cap=CAP_LAT)
        if with_ctx:
            slot_c = _select(aff_rows, tri, n=CTX_LEN, col_blk=SEQ // CTX_LEN,
                             cap=CAP_CTX).reshape(B, N_EXPERTS, CTX_LEN)
            y_ctx = _ffn_ctx(u2, per_expert(slot_c), l, w_gate, w_up, w_down)
            h_ctx = _scatter(h1, mod_all, l, slot_c, aff_em, y_ctx, ln2_g, ln2_b, n=CTX_LEN,
                             tile=CTX_LEN, row_blk0=SEQ // CTX_LEN, mod_row=B, cap=CAP_CTX)
    return h_lat
```

```python
import functools

import numpy as np
import jax
import jax.numpy as jnp
from jax import lax
from jax.experimental import pallas as pl
from jax.experimental.pallas import tpu as pltpu

D_MODEL = 1024
SEQ = 2048
DEPTH = 2
CTX_LEN = 256
GRID_W = 64
HEAD_DIM = 64
N_Q_HEADS = 4
N_KV_HEADS = 2
Q_W = 256
KV_W = 128
F_W = 256
FOURIER_GROUP_W = 64
N_BRANCH = 4
MIX_W = 3 * (Q_W + 2 * KV_W) + F_W
GATE_W = N_BRANCH * D_MODEL
WINDOW = 128
NA_KH = 8
NA_KW = 16
ROPE_THETA = 10000.0
N_EXPERTS = 16
EC_CAPACITY = 2
N_MOD = 6
DN_ALPHA = (2 * DEPTH) ** 0.25
LN_EPS = 1e-6
RMS_EPS = 1e-6
NEG_INF = -1e30
LOG2E = 1.4426950408889634

N_TOK = SEQ + CTX_LEN
TM = 256
QB = 128
NQ = 2
NQ_LAT = 4
TM_LAT = 256
MIXER_KINDS = ("global", "window", "na")
TM_SCATTER = 512
N_LAT_T = SEQ // TM
N_LAT_QB = SEQ // QB
WIN_SPAN = QB + 2 * WINDOW
NA_SPAN = 640
NA_TYPES = 5
CAP_LAT = EC_CAPACITY * SEQ // N_EXPERTS
CAP_CTX = EC_CAPACITY * CTX_LEN // N_EXPERTS
R_LANES = 128

BF = jnp.bfloat16
F32 = jnp.float32
VMEM_LIMIT = 56 * 1024 * 1024


def _cparams(n_axes):
    return pltpu.CompilerParams(dimension_semantics=("arbitrary",) * n_axes,
                                vmem_limit_bytes=VMEM_LIMIT)


def _dot(a, b):
    return jnp.dot(a, b, preferred_element_type=F32)


def _dot_nt(a, b):
    return lax.dot_general(a, b, (((1,), (1,)), ((), ())), preferred_element_type=F32)


def _sigmoid(x):
    return 0.5 * jnp.tanh(0.5 * x) + 0.5


def _silu(x):
    return x * _sigmoid(x)


def _layer_norm(x, g, b):
    mu = jnp.mean(x, axis=-1, keepdims=True)
    xc = x - mu
    var = jnp.mean(xc * xc, axis=-1, keepdims=True)
    return xc * lax.rsqrt(var + LN_EPS) * g + b


def _mod_kernel(c_ref, w_ref, b_ref, o_ref):
    c = c_ref[...]
    a = c * jax.nn.sigmoid(c)
    o_ref[...] = jnp.dot(a, w_ref[...], precision=lax.Precision.HIGHEST,
                         preferred_element_type=F32) + b_ref[...]


def _modulation(c_all, w_mod, b_mod):
    L = w_mod.shape[0]
    R = c_all.shape[0]
    ncol = N_MOD * D_MODEL
    tn = D_MODEL
    return pl.pallas_call(
        _mod_kernel,
        grid=(L, ncol // tn),
        in_specs=[pl.BlockSpec((R, D_MODEL), lambda l, j: (0, 0)),
                  pl.BlockSpec((None, D_MODEL, tn), lambda l, j: (l, 0, j)),
                  pl.BlockSpec((None, 1, tn), lambda l, j: (l, 0, j))],
        out_specs=pl.BlockSpec((None, R, tn), lambda l, j: (l, 0, j)),
        out_shape=jax.ShapeDtypeStruct((L, R, ncol), F32),
        compiler_params=_cparams(2),
        name="modulation",
    )(c_all, w_mod, b_mod.reshape(L, 1, ncol))


def _rope128(x, cos, sin):
    lane = lax.broadcasted_iota(jnp.int32, x.shape, 1)
    first = (lane % 32) < 16
    up = pltpu.roll(x, 128 - 16, axis=1)
    dn = pltpu.roll(x, 16, axis=1)
    return x * cos + jnp.where(first, up, dn) * sin


def _rms128(x, gain, bd):
    x2 = x * x
    hi = x2.astype(BF)
    lo = (x2 - hi.astype(F32)).astype(BF)
    ms = _dot(hi, bd) + _dot(lo, bd)
    return x * lax.rsqrt(ms + RMS_EPS) * gain


def _q_slabs(x0, x1):
    lane = lax.broadcasted_iota(jnp.int32, x0.shape, 1)
    lo = lane < HEAD_DIM
    x0r = pltpu.roll(x0, HEAD_DIM, axis=1)
    x1r = pltpu.roll(x1, HEAD_DIM, axis=1)
    scale = HEAD_DIM ** -0.5 * LOG2E
    slabs = [jnp.where(lo, x0, 0.0), jnp.where(lo, x0r, 0.0),
             jnp.where(lo, 0.0, x1r), jnp.where(lo, 0.0, x1)]
    return jnp.concatenate([(s * scale).astype(BF) for s in slabs], axis=1)


def _proj_kernel(hl_ref, hc_ref, mod_ref, w_ref, gq_ref, gk_ref, cos_ref, sin_ref, bd_ref,
                 q_ref, kv_ref):
    m = mod_ref[...]
    h = jnp.where(pl.program_id(1) == N_LAT_T, hc_ref[...], hl_ref[...])
    u = h * (1.0 + m[1:2]) + m[0:1]
    p = _dot(u.astype(BF), w_ref[...])
    cos = cos_ref[...]
    sin = sin_ref[...]
    bd = bd_ref[...]
    gq = gq_ref[...]
    gk = gk_ref[...]
    c0, s0 = cos[:, :128], sin[:, :128]
    c1, s1 = cos[:, 128:], sin[:, 128:]

    qa0 = _rope128(_rms128(p[:, 0:128], gq, bd), c0, s0)
    qa1 = _rope128(_rms128(p[:, 128:256], gq, bd), c1, s1)
    q_ref[:, 0:512] = _q_slabs(qa0, qa1)
    q_ref[:, 512:1024] = _q_slabs(_rope128(p[:, 768:896], c0, s0), _rope128(p[:, 896:1024], c1, s1))
    q_ref[:, 1024:1536] = _q_slabs(p[:, 1280:1408], p[:, 1408:1536])
    kv_ref[:, 0:128] = _rope128(_rms128(p[:, 256:384], gk, bd), c0, s0).astype(BF)
    kv_ref[:, 128:256] = p[:, 384:512].astype(BF)
    kv_ref[:, 256:384] = _rope128(p[:, 1024:1152], c0, s0).astype(BF)
    kv_ref[:, 384:512] = p[:, 1152:1280].astype(BF)
    kv_ref[:, 512:768] = p[:, 1536:1792].astype(BF)
    kv_ref[:, 768:1024] = p[:, 512:768].astype(BF)


QS_W = 2 * Q_W
KV_COL = {"global": (0, 1), "window": (2, 3), "na": (4, 5)}
Q_COL = {"global": 0, "window": 1, "na": 2}
ACT_W = 1024
FB_COL = 3


def _lat_ctx_specs(h_lat, h_ctx):
    return ([pl.BlockSpec((None, TM, D_MODEL), lambda b, t: (b, jnp.minimum(t, N_LAT_T - 1), 0)),
             pl.BlockSpec((None, TM, D_MODEL), lambda b, t: (b, 0, 0))], [h_lat, h_ctx])


def _project(h_lat, h_ctx, mod, l, w_mix, gq, gk, cos_t, sin_t, bd):
    B = h_lat.shape[0]
    n_t = N_TOK // TM
    ctx_row = B
    tok = lambda w: pl.BlockSpec((None, TM, w), lambda b, t: (b, t, 0))
    const = lambda shape: pl.BlockSpec(shape, lambda b, t: (0,) * len(shape))
    widths = (3 * QS_W, ACT_W)
    h_specs, h_args = _lat_ctx_specs(h_lat, h_ctx)
    return pl.pallas_call(
        _proj_kernel,
        grid=(B, n_t),
        in_specs=h_specs + [
            pl.BlockSpec((None, None, N_MOD, D_MODEL),
                         lambda b, t: (l, jnp.where(t == N_LAT_T, ctx_row, b), 0, 0)),
            pl.BlockSpec((None, D_MODEL, MIX_W), lambda b, t: (l, 0, 0)),
            const((1, 128)), const((1, 128)),
            pl.BlockSpec((TM, Q_W), lambda b, t: (t, 0)),
            pl.BlockSpec((TM, Q_W), lambda b, t: (t, 0)),
            const((128, 128))],
        out_specs=[tok(w) for w in widths],
        out_shape=[jax.ShapeDtypeStruct((B, N_TOK, w), BF) for w in widths],
        compiler_params=_cparams(2),
        name="mixer_projection",
    )(*h_args, mod, w_mix, gq, gk, cos_t, sin_t, bd)


def _fourier_kernel(x_ref, ce_ref, co_ref, cc_ref, sc_ref, o_ref, zp_ref, zm_ref, *, row0, n):
    half = n // 2

    @pl.when(pl.program_id(1) == 0)
    def _():
        x = x_ref[row0:row0 + n, :]
        for part, w_ref in enumerate((cc_ref, sc_ref)):
            z = _dot(x, w_ref[...])
            z = z if part == 0 else -z
            rows = slice(part * half, (part + 1) * half)
            zp_ref[rows, :] = (z[:half] + z[half:]).astype(BF)
            zm_ref[rows, :] = (z[:half] - z[half:]).astype(BF)

    scale = (n * FOURIER_GROUP_W) ** -0.5
    o_ref[:, 0:F_W] = (_dot(ce_ref[...], zp_ref[...]) * scale).astype(BF)
    o_ref[:, F_W:2 * F_W] = (_dot(co_ref[...], zm_ref[...]) * scale).astype(BF)


def _fourier(fb_all, cs, cc, sc, *, row0, n, tq):
    B = fb_all.shape[0]
    half = n // 2
    out = pl.pallas_call(
        functools.partial(_fourier_kernel, row0=row0, n=n),
        grid=(B, half // tq),
        in_specs=[pl.BlockSpec((None, N_TOK, F_W), lambda b, t: (b, 0, FB_COL)),
                  pl.BlockSpec((None, tq, n), lambda b, t: (0, t, 0)),
                  pl.BlockSpec((None, tq, n), lambda b, t: (1, t, 0)),
                  pl.BlockSpec((F_W, F_W), lambda b, t: (0, 0)),
                  pl.BlockSpec((F_W, F_W), lambda b, t: (0, 0))],
        out_specs=pl.BlockSpec((None, tq, 2 * F_W), lambda b, t: (b, t, 0)),
        out_shape=jax.ShapeDtypeStruct((B, half, 2 * F_W), BF),
        scratch_shapes=[pltpu.VMEM((n, F_W), BF), pltpu.VMEM((n, F_W), BF)],
        compiler_params=_cparams(2),
        name="fourier_mixer",
    )(fb_all, cs, cs, cc, sc)
    return out.reshape(B, n, F_W)


def _attend(q, kcat, vcat, add_bias, sinks):
    outs = []
    for h in range(N_KV_HEADS):
        qs = jnp.concatenate([q[:, (2 * h) * 128:(2 * h + 1) * 128],
                              q[:, (2 * h + 1) * 128:(2 * h + 2) * 128]], axis=0)
        s = add_bias(_dot_nt(qs, kcat), h)
        m = jnp.max(s, axis=-1, keepdims=True)
        if sinks is not None:
            row = lax.broadcasted_iota(jnp.int32, (2 * QB, 1), 0)
            sk = jnp.where(row < QB, sinks[2 * h], sinks[2 * h + 1])
            m = jnp.maximum(m, sk)
        p = jnp.exp2(s - m)
        l = jnp.sum(p, axis=-1, keepdims=True)
        if sinks is not None:
            l = l + jnp.exp2(sk - m)
        outs.append(_dot(p.astype(BF), vcat) / l)
    o0, o1 = outs
    lane = lax.broadcasted_iota(jnp.int32, (QB, 128), 1)
    lo = lane < HEAD_DIM
    left = jnp.where(lo, o0[:QB], pltpu.roll(o0[QB:], HEAD_DIM, axis=1))
    right = jnp.where(lo, pltpu.roll(o1[:QB], HEAD_DIM, axis=1), o1[QB:])
    return jnp.concatenate([left, right], axis=1)


def _mixers_kernel(*refs, kinds, with_ctx, nq):
    refs = list(refs)
    sinks = None
    if "window" in kinds:
        sink_ref = refs.pop(0)
        sinks = [sink_ref[i] * LOG2E for i in range(N_Q_HEADS)]
    io = {}
    for kind in kinds:
        io[kind] = [refs.pop(0), refs.pop(0), refs.pop(0)]
    if "na" in kinds:
        tab_ref = refs.pop(0)
    out_ref = refs.pop(0)
    for n, kind in enumerate(kinds):
        io[kind].append(out_ref.at[:, n * Q_W:(n + 1) * Q_W])
    step = pl.program_id(1)
    no_bias = lambda s, h: s

    def lat_block(kind, j, q, k_ref, v_ref):
        ctx_k, ctx_v = k_ref[SEQ:N_TOK, :], v_ref[SEQ:N_TOK, :]
        if kind == "global":
            return _attend(q, k_ref[...], v_ref[...], no_bias, None)
        if kind == "window":
            start = pl.multiple_of(QB * jnp.clip(j - 1, 0, N_LAT_QB - 3), QB)
            kcat = jnp.concatenate([k_ref[pl.ds(start, WIN_SPAN), :], ctx_k], axis=0)
            vcat = jnp.concatenate([v_ref[pl.ds(start, WIN_SPAN), :], ctx_v], axis=0)
            nk = WIN_SPAN + CTX_LEN
            col = lax.broadcasted_iota(jnp.int32, (2 * QB, nk), 1)
            row = lax.broadcasted_iota(jnp.int32, (2 * QB, nk), 0)
            q_pos = QB * j + row % QB
            ok = (col >= WIN_SPAN) | (jnp.abs(q_pos - (start + col)) <= WINDOW)
            return _attend(q, kcat, vcat, lambda s, h: jnp.where(ok, s, NEG_INF), sinks)
        start = pl.multiple_of(QB * jnp.clip(j - 2, 0, N_LAT_QB - 5), QB)
        kcat = jnp.concatenate([k_ref[pl.ds(start, NA_SPAN), :], ctx_k], axis=0)
        vcat = jnp.concatenate([v_ref[pl.ds(start, NA_SPAN), :], ctx_v], axis=0)
        tix = jnp.where(j < 2, j, jnp.where(j > N_LAT_QB - 3, j - (N_LAT_QB - 5), 2))

        def na_bias(s, h):
            b = tab_ref[tix, 2 * h:2 * h + 2]
            return s + b.reshape(2 * QB, NA_SPAN + CTX_LEN)

        return _attend(q, kcat, vcat, na_bias, None)

    def latent():
        for i in range(nq):
            rows = slice(i * QB, (i + 1) * QB)
            for kind in kinds:
                q_ref, k_ref, v_ref, o_ref = io[kind]
                o_ref[rows, :] = lat_block(kind, step * nq + i, q_ref[rows, :], k_ref, v_ref).astype(BF)

    def context():
        for i in range(nq):
            rows = slice(i * QB, (i + 1) * QB)
            for kind in kinds:
                q_ref, k_ref, v_ref, o_ref = io[kind]
                o = _attend(q_ref[rows, :], k_ref[SEQ:N_TOK, :], v_ref[SEQ:N_TOK, :], no_bias,
                            sinks if kind == "window" else None)
                o_ref[rows, :] = o.astype(BF)

    if with_ctx:
        pl.when(step < N_LAT_QB // nq)(latent)
        pl.when(step >= N_LAT_QB // nq)(context)
    else:
        latent()


def _mixers(kinds, q_all, kv_all, sink, na_table, with_ctx):
    B = q_all.shape[0]
    nq = NQ if with_ctx else NQ_LAT
    n_steps = (N_LAT_QB + (CTX_LEN // QB if with_ctx else 0)) // nq
    in_specs, args = [], []
    if "window" in kinds:
        in_specs.append(pl.BlockSpec(memory_space=pltpu.SMEM))
        args.append(sink)
    for kind in kinds:
        qc, (kc, vc) = Q_COL[kind], KV_COL[kind]
        in_specs += [pl.BlockSpec((None, nq * QB, QS_W), lambda b, j, c=qc: (b, j, c)),
                     pl.BlockSpec((None, N_TOK, KV_W), lambda b, j, c=kc: (b, 0, c)),
                     pl.BlockSpec((None, N_TOK, KV_W), lambda b, j, c=vc: (b, 0, c))]
        args += [q_all, kv_all, kv_all]
    if "na" in kinds:
        in_specs.append(pl.BlockSpec(na_table.shape, lambda b, j: (0, 0, 0, 0)))
        args.append(na_table)
    width = Q_W * len(kinds)
    return pl.pallas_call(
        functools.partial(_mixers_kernel, kinds=kinds, with_ctx=with_ctx, nq=nq),
        grid=(B, n_steps),
        in_specs=in_specs,
        out_specs=pl.BlockSpec((None, nq * QB, width), lambda b, j: (b, j, 0)),
        out_shape=jax.ShapeDtypeStruct((B, n_steps * nq * QB, width), BF),
        compiler_params=_cparams(2),
        name="attention_" + "_".join(kinds),
    )(*args)


def _merge_kernel(*refs, with_ctx):
    if with_ctx:
        (h_ref, hc_ref, mod_ref, acd_ref, bl_ref, bc_ref,
         wg_ref, wb_ref, wo_ref, g_ref, b_ref, wr_ref, h1_ref, aff_ref, u_ref) = refs
    else:
        (h_ref, mod_ref, acd_ref, bl_ref,
         wg_ref, wb_ref, wo_ref, g_ref, b_ref, wr_ref, h1_ref, aff_ref) = refs
        u_ref = None
    m = mod_ref[...]
    h = h_ref[...]
    br_b = bl_ref[...]
    if with_ctx:
        is_ctx = pl.program_id(1) == N_LAT_T
        h = jnp.where(is_ctx, hc_ref[...], h)
        br_b = jnp.where(is_ctx, bc_ref[...], br_b)
    u = (h * (1.0 + m[1:2]) + m[0:1]).astype(BF)
    acd = acd_ref[...]
    branches = (acd[:, 0:Q_W], br_b, acd[:, Q_W:2 * Q_W], acd[:, 2 * Q_W:3 * Q_W])
    merged = None
    for i in range(N_BRANCH):
        gate = _sigmoid(_dot(u, wg_ref[:, i * D_MODEL:(i + 1) * D_MODEL]))
        term = gate * _dot(branches[i], wb_ref[i])
        merged = term if merged is None else merged + term
    y = _dot(merged.astype(BF), wo_ref[...])
    h1 = _layer_norm(DN_ALPHA * h + m[2:3] * y, g_ref[...], b_ref[...])
    h1_ref[...] = h1

    u2 = h1 * (1.0 + m[4:5]) + m[3:4]
    u_hi = u2.astype(BF)
    if u_ref is not None:
        u_ref[...] = u_hi
    u_lo = (u2 - u_hi.astype(F32)).astype(BF)
    part = _dot(u_hi, wr_ref[...]) + _dot(u_lo, wr_ref[...])
    logits = part + pltpu.roll(part, R_LANES - N_EXPERTS, axis=1)
    lane = lax.broadcasted_iota(jnp.int32, logits.shape, 1)
    lg = jnp.where(lane < N_EXPERTS, logits, NEG_INF)
    ex = jnp.exp(lg - jnp.max(lg, axis=-1, keepdims=True))
    aff = ex / jnp.sum(ex, axis=-1, keepdims=True)
    aff_ref[...] = aff.T[0:N_EXPERTS]


def _merge(h_lat, h_ctx, mod, l, br_acd, brb_lat, brb_ctx, w_gate, w_branch, w_out, ln_g, ln_b, w_router_split):
    B = h_lat.shape[0]
    with_ctx = brb_ctx is not None
    tm = TM if with_ctx else TM_LAT
    n_lat_t = SEQ // tm
    n_t = n_lat_t + (1 if with_ctx else 0)
    tok = lambda w: pl.BlockSpec((None, tm, w), lambda b, t: (b, t, 0))
    layer = lambda shape: pl.BlockSpec((None,) + shape, lambda b, t: (l,) + (0,) * len(shape))
    if with_ctx:
        in_specs, args = _lat_ctx_specs(h_lat, h_ctx)
    else:
        in_specs, args = [tok(D_MODEL)], [h_lat]
    in_specs += [pl.BlockSpec((None, None, N_MOD, D_MODEL),
                              lambda b, t: (l, jnp.where(t == n_lat_t, B, b), 0, 0)),
                 tok(3 * Q_W),
                 pl.BlockSpec((None, tm, F_W), lambda b, t: (b, jnp.minimum(t, n_lat_t - 1), 0))]
    args += [mod, br_acd, brb_lat]
    if with_ctx:
        in_specs.append(pl.BlockSpec((None, tm, F_W), lambda b, t: (b, 0, 0)))
        args.append(brb_ctx)
    in_specs += [layer((D_MODEL, GATE_W)), layer((N_BRANCH, Q_W, D_MODEL)),
                 layer((D_MODEL, D_MODEL)), layer((1, D_MODEL)), layer((1, D_MODEL)),
                 layer((D_MODEL, R_LANES))]
    args += [w_gate, w_branch, w_out, ln_g, ln_b, w_router_split]
    rows = n_t * tm
    out_specs = [tok(D_MODEL), pl.BlockSpec((None, N_EXPERTS, tm), lambda b, t: (b, 0, t))]
    out_shape = [jax.ShapeDtypeStruct((B, rows, D_MODEL), F32),
                 jax.ShapeDtypeStruct((B, N_EXPERTS, rows), F32)]
    if with_ctx:
        out_specs.append(tok(D_MODEL))
        out_shape.append(jax.ShapeDtypeStruct((B, rows, D_MODEL), BF))
    out = pl.pallas_call(
        functools.partial(_merge_kernel, with_ctx=with_ctx),
        grid=(B, n_t),
        in_specs=in_specs,
        out_specs=out_specs,
        out_shape=out_shape,
        compiler_params=_cparams(2),
        name="gated_merge",
    )(*args)
    return out if with_ctx else (out[0], out[1], None)


def _excl_cumsum_lanes(x, tri):
    R, n = x.shape
    nc = n // 128
    xs = jnp.concatenate([x[:, c * 128:(c + 1) * 128] for c in range(nc)], axis=0)
    incl = _dot(xs.astype(BF), tri)
    excl = incl - xs
    tot = incl[:, 127:128]
    off = jnp.zeros((R, 1), F32)
    pieces = []
    for c in range(nc):
        pieces.append(excl[c * R:(c + 1) * R] + off)
        off = off + tot[c * R:(c + 1) * R]
    return jnp.concatenate(pieces, axis=1)


MIN_NORMAL = 2.0 ** -126
MANTISSA_STEPS = 30


def _select_top(a, cap, tri):
    def count_ge(c):
        return jnp.sum(jnp.where(a >= c, 1.0, 0.0), axis=-1, keepdims=True)

    has = count_ge(MIN_NORMAL) >= cap
    p = jnp.full((a.shape[0], 1), MIN_NORMAL, F32)
    for s in (64, 32, 16, 8, 4, 2, 1):
        cand = p * (2.0 ** s)
        p = jnp.where(count_ge(cand) >= cap, cand, p)
    lo = jnp.where(has, p, 0.0)
    hi = jnp.where(has, 2.0 * p, MIN_NORMAL)
    for _ in range(MANTISSA_STEPS):
        mid = 0.5 * (lo + hi)
        ok = count_ge(mid) >= cap
        lo = jnp.where(ok, mid, lo)
        hi = jnp.where(ok, hi, mid)
    above = a >= hi
    edge = jnp.where(a >= lo, jnp.where(above, 0.0, 1.0), 0.0)
    need = cap - jnp.sum(jnp.where(above, 1.0, 0.0), axis=-1, keepdims=True)
    edge_rank = _excl_cumsum_lanes(edge, tri)
    sel = jnp.where(above, 1.0, jnp.where(edge_rank < need, edge, 0.0))
    return sel > 0.5, _excl_cumsum_lanes(sel, tri)


def _select_kernel(aff_ref, tri_ref, slot_ref, *, cap):
    sel, rank = _select_top(aff_ref[...], cap, tri_ref[...])
    slot_ref[...] = jnp.where(sel, rank, -1.0).astype(jnp.int32)


def _select(aff_rows, tri, *, n, col_blk, cap):
    R = aff_rows.shape[0]
    return pl.pallas_call(
        functools.partial(_select_kernel, cap=cap),
        grid=(1,),
        in_specs=[pl.BlockSpec((R, n), lambda i: (0, col_blk)),
                  pl.BlockSpec((128, 128), lambda i: (0, 0))],
        out_specs=pl.BlockSpec((R, n), lambda i: (0, 0)),
        out_shape=jax.ShapeDtypeStruct((R, n), jnp.int32),
        compiler_params=_cparams(1),
        name="ec_select",
    )(aff_rows, tri)


def _swiglu(xs, wg, wu, wd):
    a = _dot(xs, wg)
    up = _dot(xs, wu)
    hm = _silu(a) * up
    return _dot(hm.astype(BF), wd)


def _index_kernel(slot_ref, idx_ref):
    n = slot_ref.shape[1]
    cap = idx_ref.shape[1]
    slot_tm = _token_major(slot_ref[...].astype(F32)).astype(jnp.int32)
    tok = lax.broadcasted_iota(jnp.int32, (n, cap), 0).astype(F32)
    lane = lax.broadcasted_iota(jnp.int32, (n, cap), 1)
    rows = [jnp.sum(jnp.where(slot_tm[:, e:e + 1] == lane, tok, 0.0), axis=0, keepdims=True)
            for e in range(N_EXPERTS)]
    idx_ref[...] = jnp.concatenate(rows, axis=0).astype(jnp.int32)


def _slot_tokens(slot, cap):
    B, _, n = slot.shape
    return pl.pallas_call(
        _index_kernel,
        grid=(B,),
        in_specs=[pl.BlockSpec((None, N_EXPERTS, n), lambda b: (b, 0, 0))],
        out_specs=pl.BlockSpec((None, N_EXPERTS, cap), lambda b: (b, 0, 0)),
        out_shape=jax.ShapeDtypeStruct((B, N_EXPERTS, cap), jnp.int32),
        compiler_params=_cparams(1),
        name="ec_slot_tokens",
    )(slot)


FFN_CHUNKS = 4


def _gather_copy(h_hbm, xbuf, sem, b, tok, buf, j):
    return pltpu.make_async_copy(h_hbm.at[b, pl.ds(tok, 1), :], xbuf.at[buf, pl.ds(j, 1), :],
                                 sem.at[buf])


FFN_GROUP = 2
GATHER_AHEAD = 2
N_GATHER_BUF = GATHER_AHEAD + 1


def _ffn_lat_kernel(idx0_ref, idx1_ref, idx2_ref, h_hbm, mod_ref, wg_ref, wu_ref, wd_ref, y_ref,
                    wg_s, wu_s, wd_s, xbuf, sem):
    e, b = pl.program_id(0), pl.program_id(1)
    n_b = pl.num_programs(1)
    step = e * n_b + b
    buf = step % N_GATHER_BUF
    group = idx0_ref.shape[0]
    n_rows = group * CAP_LAT

    def start_rows(idx, ahead, lo=0, hi=n_rows):
        first, into = ((b + ahead) % n_b) * group, (step + ahead) % N_GATHER_BUF
        for r in range(lo, hi):
            s, j = divmod(r, CAP_LAT)
            _gather_copy(h_hbm, xbuf, sem, first + s, idx[s, 0, j], into, r).start(priority=r % 2)

    def wait_rows(which):
        for r in range(n_rows):
            _gather_copy(h_hbm, xbuf, sem, 0, 0, which, r).wait()

    @pl.when(step == 0)
    def _():
        start_rows(idx0_ref, 0)
        start_rows(idx1_ref, 1)

    @pl.when(b == 0)
    def _():
        wg_s[...] = wg_ref[...].astype(BF)
        wu_s[...] = wu_ref[...].astype(BF)
        wd_s[...] = wd_ref[...].astype(BF)

    wait_rows(buf)
    xs = []
    for s in range(group):
        m = mod_ref[s]
        rows = xbuf[buf, s * CAP_LAT:(s + 1) * CAP_LAT, :]
        xs.append((rows * (1.0 + m[4:5]) + m[3:4]).astype(BF))
    xs = jnp.concatenate(xs, axis=0)
    fc = wg_s.shape[1] // FFN_CHUNKS
    per = n_rows // FFN_CHUNKS
    y = None
    for c in range(FFN_CHUNKS):
        start_rows(idx2_ref, GATHER_AHEAD, c * per, (c + 1) * per)
        a = _dot(xs, wg_s[:, c * fc:(c + 1) * fc])
        up = _dot(xs, wu_s[:, c * fc:(c + 1) * fc])
        hm = (_silu(a) * up).astype(BF)
        part = _dot(hm, wd_s[c * fc:(c + 1) * fc, :])
        y = part if y is None else y + part
    for s in range(group):
        y_ref[s] = y[s * CAP_LAT:(s + 1) * CAP_LAT].astype(BF)

    @pl.when(step + 1 == pl.num_programs(0) * n_b)
    def _():
        for ahead in range(1, N_GATHER_BUF):
            wait_rows((step + ahead) % N_GATHER_BUF)


def _onehot_rows(slot, u, cap):
    n = slot.shape[1]
    hit = slot == lax.broadcasted_iota(jnp.int32, (cap, n), 0)
    return _dot(jnp.where(hit, 1.0, 0.0).astype(BF), u).astype(BF)


def _ffn_ctx_kernel(u_ref, slot_ref, wg_ref, wu_ref, wd_ref, y_ref):
    B = u_ref.shape[0]
    xs = jnp.concatenate([_onehot_rows(slot_ref[b], u_ref[b], CAP_CTX) for b in range(B)], axis=0)
    y = _swiglu(xs, wg_ref[...].astype(BF), wu_ref[...].astype(BF), wd_ref[...].astype(BF))
    for b in range(B):
        y_ref[b] = y[b * CAP_CTX:(b + 1) * CAP_CTX].astype(BF)


def _ffn_lat(h1, idx, mod, l, wg, wu, wd):
    B = h1.shape[0]
    F = wg.shape[-1]
    group = FFN_GROUP if B % FFN_GROUP == 0 else 1
    n_g = B // group
    wspec = lambda s: pl.BlockSpec((None, None) + s, lambda e, g: (l, e, 0, 0))

    def idx_spec(ahead):
        def index_map(e, g):
            return (g + ahead) % n_g, jnp.minimum(e + (g + ahead) // n_g, N_EXPERTS - 1), 0, 0
        return pl.BlockSpec((group, None, 1, CAP_LAT), index_map, memory_space=pltpu.SMEM)

    return pl.pallas_call(
        _ffn_lat_kernel,
        grid=(N_EXPERTS, n_g),
        in_specs=[idx_spec(0), idx_spec(1), idx_spec(GATHER_AHEAD),
                  pl.BlockSpec(memory_space=pl.ANY),
                  pl.BlockSpec((None, group, N_MOD, D_MODEL), lambda e, g: (l, g, 0, 0)),
                  wspec((D_MODEL, F)), wspec((D_MODEL, F)), wspec((F, D_MODEL))],
        out_specs=pl.BlockSpec((group, CAP_LAT, D_MODEL), lambda e, g: (g, e, 0)),
        out_shape=jax.ShapeDtypeStruct((B, N_EXPERTS * CAP_LAT, D_MODEL), BF),
        scratch_shapes=[pltpu.VMEM((D_MODEL, F), BF), pltpu.VMEM((D_MODEL, F), BF),
                        pltpu.VMEM((F, D_MODEL), BF),
                        pltpu.VMEM((N_GATHER_BUF, group * CAP_LAT, D_MODEL), F32),
                        pltpu.SemaphoreType.DMA((N_GATHER_BUF,))],
        compiler_params=_cparams(2),
        name="ec_ffn_latent",
    )(idx, idx, idx, h1, mod, wg, wu, wd)


def _ffn_ctx(u, slot_em, l, wg, wu, wd):
    B = u.shape[0]
    F = wg.shape[-1]
    wspec = lambda s: pl.BlockSpec((None, None) + s, lambda e: (l, e, 0, 0))
    return pl.pallas_call(
        _ffn_ctx_kernel,
        grid=(N_EXPERTS,),
        in_specs=[pl.BlockSpec((B, CTX_LEN, D_MODEL), lambda e: (0, SEQ // CTX_LEN, 0),
                               pipeline_mode=pl.Buffered(1)),
                  pl.BlockSpec((B, None, 1, CTX_LEN), lambda e: (0, e, 0, 0)),
                  wspec((D_MODEL, F)), wspec((D_MODEL, F)), wspec((F, D_MODEL))],
        out_specs=pl.BlockSpec((B, CAP_CTX, D_MODEL), lambda e: (0, e, 0)),
        out_shape=jax.ShapeDtypeStruct((B, N_EXPERTS * CAP_CTX, D_MODEL), BF),
        compiler_params=_cparams(1),
        name="ec_ffn_context",
    )(u, slot_em, wg, wu, wd)


def _token_major(x_em):
    pad = jnp.full((R_LANES - N_EXPERTS, x_em.shape[1]), -1.0, F32)
    return jnp.concatenate([x_em, pad], axis=0).T


def _scatter_matrix(slot_tm, aff_tm, cap):
    rows = slot_tm.shape[0]
    if cap % 128 == 0:
        lane = lax.broadcasted_iota(jnp.int32, (rows, cap), 1)
        pieces = [jnp.where(slot_tm[:, e:e + 1] == lane, aff_tm[:, e:e + 1], 0.0).astype(BF)
                  for e in range(N_EXPERTS)]
        return jnp.concatenate(pieces, axis=1)
    lane = lax.broadcasted_iota(jnp.int32, (rows, N_EXPERTS * cap), 1)
    acc = jnp.zeros((rows, N_EXPERTS * cap), F32)
    for e in range(N_EXPERTS):
        s = slot_tm[:, e:e + 1]
        target = jnp.where(s >= 0, s + e * cap, -1)
        acc = acc + jnp.where(target == lane, aff_tm[:, e:e + 1], 0.0)
    return acc.astype(BF)


def _scatter_kernel(h_ref, mod_ref, slot_ref, aff_ref, y_ref, g_ref, b_ref, o_ref, *, cap):
    slot_tm = _token_major(slot_ref[...].astype(F32)).astype(jnp.int32)
    f = _dot(_scatter_matrix(slot_tm, _token_major(aff_ref[...]), cap), y_ref[...])
    m = mod_ref[...]
    o_ref[...] = _layer_norm(DN_ALPHA * h_ref[...] + m[5:6] * f, g_ref[...], b_ref[...])


def _scatter(h1, mod, l, slot, aff_em, y, ln_g, ln_b, *, n, tile, row_blk0, mod_row, cap):
    B = h1.shape[0]
    mrow = (lambda b: b) if mod_row is None else (lambda b: mod_row)
    return pl.pallas_call(
        functools.partial(_scatter_kernel, cap=cap),
        grid=(B, n // tile),
        in_specs=[pl.BlockSpec((None, tile, D_MODEL), lambda b, t: (b, row_blk0 + t, 0)),
                  pl.BlockSpec((None, None, N_MOD, D_MODEL), lambda b, t: (l, mrow(b), 0, 0)),
                  pl.BlockSpec((None, N_EXPERTS, tile), lambda b, t: (b, 0, t)),
                  pl.BlockSpec((None, N_EXPERTS, tile), lambda b, t: (b, 0, row_blk0 + t)),
                  pl.BlockSpec((None, N_EXPERTS * cap, D_MODEL), lambda b, t: (b, 0, 0)),
                  pl.BlockSpec((None, 1, D_MODEL), lambda b, t: (l, 0, 0)),
                  pl.BlockSpec((None, 1, D_MODEL), lambda b, t: (l, 0, 0))],
        out_specs=pl.BlockSpec((None, tile, D_MODEL), lambda b, t: (b, t, 0)),
        out_shape=jax.ShapeDtypeStruct((B, n, D_MODEL), F32),
        compiler_params=_cparams(2),
        name="ec_scatter_norm",
    )(h1, mod, slot, aff_em, y, ln_g, ln_b)


def _rope_tables():
    t = jnp.arange(SEQ)
    axis_dim = HEAD_DIM // 2
    inv = ROPE_THETA ** (-jnp.arange(0, axis_dim, 2, dtype=F32) / axis_dim)
    ang_r = (t // GRID_W).astype(F32)[:, None] * inv
    ang_c = (t % GRID_W).astype(F32)[:, None] * inv
    cr, sr, cc, sc = jnp.cos(ang_r), jnp.sin(ang_r), jnp.cos(ang_c), jnp.sin(ang_c)
    cos_h = jnp.concatenate([cr, cr, cc, cc], axis=-1)
    sin_h = jnp.concatenate([-sr, sr, -sc, sc], axis=-1)
    cos_t = jnp.concatenate([jnp.tile(cos_h, (1, N_Q_HEADS)), jnp.ones((CTX_LEN, Q_W), F32)], axis=0)
    sin_t = jnp.concatenate([jnp.tile(sin_h, (1, N_Q_HEADS)), jnp.zeros((CTX_LEN, Q_W), F32)], axis=0)
    return cos_t, sin_t


def _dft_tables(n):
    s = np.arange(n // 2, dtype=np.int64)
    k = 2 * s[None, :, None] + np.arange(2, dtype=np.int64)[:, None, None]
    ang = 2.0 * np.pi * ((k * s[None, None, :]) % n).astype(np.float64) / n
    cs = np.concatenate([np.cos(ang), np.sin(ang)], axis=-1).astype(np.float32)
    c = np.arange(F_W)
    same = (c[:, None] // FOURIER_GROUP_W) == (c[None, :] // FOURIER_GROUP_W)
    angc = 2.0 * np.pi * (((c[:, None] % FOURIER_GROUP_W) * (c[None, :] % FOURIER_GROUP_W))
                          % FOURIER_GROUP_W) / FOURIER_GROUP_W
    cc = np.where(same, np.cos(angc), 0.0).astype(np.float32)
    sc = np.where(same, np.sin(angc), 0.0).astype(np.float32)
    return cs, cc, sc


def _na_bias_table(rpb):
    rows = SEQ // GRID_W
    q_rows, k_rows = QB // GRID_W, NA_SPAN // GRID_W
    n_dc = GRID_W - NA_KW
    padded = jnp.pad(rpb.astype(F32) * LOG2E, ((0, 0), (0, 0), (n_dc, n_dc)))
    toe = jnp.stack([padded[:, :, GRID_W - 1 - c:2 * GRID_W - 1 - c] for c in range(GRID_W)], axis=2)
    zero_blk = jnp.zeros((N_Q_HEADS, GRID_W, GRID_W), F32)
    col = np.arange(GRID_W)
    c0 = np.clip(col - NA_KW // 2, 0, GRID_W - NA_KW)
    col_ok = (col[None, :] >= c0[:, None]) & (col[None, :] < c0[:, None] + NA_KW)
    tables = []
    for j in (0, 1, 2, N_LAT_QB - 2, N_LAT_QB - 1):
        kb = min(max(j - 2, 0), N_LAT_QB - 5)
        q_parts = []
        for qr in range(q_rows):
            r = q_rows * j + qr
            r0 = min(max(r - NA_KH // 2, 0), rows - NA_KH)
            k_parts = []
            for ki in range(k_rows):
                kr = q_rows * kb + ki
                if r0 <= kr < r0 + NA_KH:
                    k_parts.append(jnp.where(col_ok[None], toe[:, kr - r + NA_KH - 1], NEG_INF))
                else:
                    k_parts.append(zero_blk + NEG_INF)
            q_parts.append(jnp.concatenate(k_parts, axis=-1))
        tables.append(jnp.concatenate(q_parts, axis=-2))
    tab = jnp.stack(tables, axis=0)
    return jnp.concatenate([tab, jnp.zeros(tab.shape[:3] + (CTX_LEN,), F32)], axis=-1)


def kernel(x, c, ctx, c_ctx, w_mod, b_mod, w_in, qk_gain, sink_logit, na_rpb, w_branch, w_out,
           ln1_g, ln1_b, w_router, w_gate, w_up, w_down, ln2_g, ln2_b):
    B = x.shape[0]
    L = w_mod.shape[0]
    assert x.shape[1:] == (SEQ, D_MODEL) and ctx.shape[1:] == (CTX_LEN, D_MODEL)

    n_rows = -(-(B + 1) // 8) * 8
    c_all = jnp.concatenate([c, c_ctx[None, :], jnp.zeros((n_rows - B - 1, D_MODEL), F32)], axis=0)
    mod_all = _modulation(c_all, w_mod, b_mod).reshape(L, n_rows, N_MOD, D_MODEL)

    cos_t, sin_t = _rope_tables()
    gidx = np.arange(128)
    bd = jnp.asarray(np.where((gidx[:, None] // HEAD_DIM) == (gidx[None, :] // HEAD_DIM),
                              1.0 / HEAD_DIM, 0.0), BF)
    tri = jnp.asarray(gidx[:, None] <= gidx[None, :], BF)
    cs_lat, cc, sc = _dft_tables(SEQ)
    cs_lat, cc, sc = jnp.asarray(cs_lat).astype(BF), jnp.asarray(cc).astype(BF), jnp.asarray(sc).astype(BF)
    cs_ctx = jnp.asarray(_dft_tables(CTX_LEN)[0]).astype(BF)

    w_mix_bf = w_in[:, :, :MIX_W].astype(BF)
    w_gatecols_bf = w_in[:, :, MIX_W:].astype(BF)
    w_branch_bf = w_branch.astype(BF)
    w_out_bf = w_out.astype(BF)
    wr_hi = w_router.astype(BF)
    wr_lo = (w_router - wr_hi.astype(F32)).astype(BF)
    wr_split = jnp.concatenate(
        [wr_hi, wr_lo, jnp.zeros((L, D_MODEL, R_LANES - 2 * N_EXPERTS), BF)], axis=-1)
    ln1_g, ln1_b, ln2_g, ln2_b = (a[:, None, :] for a in (ln1_g, ln1_b, ln2_g, ln2_b))
    per_expert = lambda a: a.reshape(a.shape[0], N_EXPERTS, 1, a.shape[-1])

    h_lat, h_ctx = x, ctx
    for l in range(L):
        with_ctx = l < L - 1
        gq = jnp.tile(qk_gain[l, 0], 2)[None, :]
        gk = jnp.tile(qk_gain[l, 1], 2)[None, :]
        q_all, kv_all = _project(h_lat, h_ctx, mod_all, l, w_mix_bf, gq, gk, cos_t, sin_t, bd)

        br_acd = _mixers(MIXER_KINDS, q_all, kv_all, sink_logit[l], _na_bias_table(na_rpb[l]), with_ctx)
        brb_lat = _fourier(kv_all, cs_lat, cc, sc, row0=0, n=SEQ, tq=512)
        brb_ctx = (_fourier(kv_all, cs_ctx, cc, sc, row0=SEQ, n=CTX_LEN, tq=CTX_LEN // 2)
                   if with_ctx else None)

        h1, aff_em, u2 = _merge(h_lat, h_ctx, mod_all, l, br_acd, brb_lat, brb_ctx,
                                w_gatecols_bf, w_branch_bf, w_out_bf, ln1_g, ln1_b, wr_split)
        aff_rows = aff_em.reshape(B * N_EXPERTS, aff_em.shape[-1])

        slot = _select(aff_rows, tri, n=SEQ, col_blk=0, cap=CAP_LAT).reshape(B, N_EXPERTS, SEQ)
        y_lat = _ffn_lat(h1, per_expert(_slot_tokens(slot, CAP_LAT)), mod_all, l,
                         w_gate, w_up, w_down)
        h_lat = _scatter(h1, mod_all, l, slot, aff_em, y_lat, ln2_g, ln2_b,
                         n=SEQ, tile=TM_SCATTER, row_blk0=0, mod_row=None, cap=CAP_LAT)
        if with_ctx:
            slot_c = _select(aff_rows, tri, n=CTX_LEN, col_blk=SEQ // CTX_LEN,
                             cap=CAP_CTX).reshape(B, N_EXPERTS, CTX_LEN)
            y_ctx = _ffn_ctx(u2, per_expert(slot_c), l, w_gate, w_up, w_down)
            h_ctx = _scatter(h1, mod_all, l, slot_c, aff_em, y_ctx, ln2_g, ln2_b, n=CTX_LEN,
                             tile=CTX_LEN, row_blk0=SEQ // CTX_LEN, mod_row=B, cap=CAP_CTX)
    return h_lat
```
